```python
import jax
import jax.numpy as jnp
from jax import lax
import numpy as np

D_MODEL = 1024
BATCH = 4
SEQ = 8192
DEPTH = 4

D_MIX = D_MODEL
GROUP_WIDTH = D_MIX // 4
MLA_V_DIM = 64
MLA_HEADS = GROUP_WIDTH // MLA_V_DIM
MLA_NOPE_DIM = 64
MLA_ROPE_DIM = 32
MLA_QK_DIM = MLA_NOPE_DIM + MLA_ROPE_DIM
MLA_Q_LORA = 3 * D_MODEL // 16
MLA_KV_LORA = D_MODEL // 8
ROPE_THETA = 10000.0
ATTN_BLOCK = 128
POOL_WINDOWS = (2, 4, 8, 16)
POOL_GROUP = GROUP_WIDTH // len(POOL_WINDOWS)
HGRN_DK = 64
HGRN_DV = 64
HGRN_HEADS = GROUP_WIDTH // HGRN_DV
HGRN_CHUNK = 128
CONV_CH = GROUP_WIDTH
CONV_K = 31
N_EXPERT_GROUPS = 4
EXPERTS_PER_GROUP = 8
N_EXPERTS = N_EXPERT_GROUPS * EXPERTS_PER_GROUP
TOP_K = 2
D_EXPERT = D_MODEL // 2
MOE_BLOCK = 128
EPS = 1e-6
IN_SIZES = (MLA_Q_LORA, MLA_KV_LORA, MLA_ROPE_DIM, GROUP_WIDTH,
            HGRN_HEADS * HGRN_DK, HGRN_HEADS * HGRN_DK, HGRN_HEADS * HGRN_DV, HGRN_HEADS * HGRN_DV,
            2 * CONV_CH)
IN_COLS = sum(IN_SIZES)
IN_OFFSETS = tuple(int(v) for v in np.cumsum(IN_SIZES)[:-1])

kernel_name = 'hybrid_parallel_heads_hmoe'


def rms_norm(x, g):
    xf = x.astype(jnp.float32)
    y = xf * lax.rsqrt(jnp.mean(xf * xf, axis=-1, keepdims=True) + EPS)
    return (y * g.astype(jnp.float32)).astype(x.dtype)


def layer_norm(x, g, b):
    xf = x.astype(jnp.float32)
    mu = jnp.mean(xf, axis=-1, keepdims=True)
    var = jnp.mean(jnp.square(xf - mu), axis=-1, keepdims=True)
    y = (xf - mu) * lax.rsqrt(var + EPS)
    return (y * g.astype(jnp.float32) + b.astype(jnp.float32)).astype(x.dtype)


def rope_tables(positions):
    half = MLA_ROPE_DIM // 2
    inv_freq = ROPE_THETA ** (-jnp.arange(half, dtype=jnp.float32) / half)
    ang = positions.astype(jnp.float32)[..., None] * inv_freq
    return jnp.cos(ang)[:, :, None, :], jnp.sin(ang)[:, :, None, :]


def apply_rope(x, cos, sin):
    x1, x2 = jnp.split(x, 2, axis=-1)
    cos = cos.astype(x.dtype)
    sin = sin.astype(x.dtype)
    return jnp.concatenate([x1 * cos - x2 * sin, x1 * sin + x2 * cos], axis=-1)


def causal_attention(q, k, v):
    B, S, H, Dq = q.shape
    nb = S // ATTN_BLOCK
    qb = q.reshape(B, nb, ATTN_BLOCK, H, Dq).transpose(1, 0, 2, 3, 4)
    starts = jnp.arange(nb, dtype=jnp.int32) * ATTN_BLOCK
    kpos = jnp.arange(S, dtype=jnp.int32)
    scale = MLA_QK_DIM ** -0.5

    def block(args):
        qi, st = args
        s = jnp.einsum('bqhd,bkhd->bhqk', qi, k).astype(jnp.float32) * scale
        qpos = st + jnp.arange(ATTN_BLOCK, dtype=jnp.int32)
        s = jnp.where(kpos[None, :] <= qpos[:, None], s, -1e30)
        p = jax.nn.softmax(s, axis=-1).astype(v.dtype)
        return jnp.einsum('bhqk,bkhd->bqhd', p, v)

    o = lax.map(block, (qb, starts))
    return o.transpose(1, 0, 2, 3, 4).reshape(B, S, H, v.shape[-1])


def mla_mixer(cq, ckv, krope, cos, sin, q_a_g, w_uq, kv_a_g, w_ukv, q_g, k_g):
    B, S, _ = cq.shape
    q = (rms_norm(cq, q_a_g) @ w_uq).reshape(B, S, MLA_HEADS, MLA_QK_DIM)
    kv = (rms_norm(ckv, kv_a_g) @ w_ukv).reshape(B, S, MLA_HEADS, MLA_NOPE_DIM + MLA_V_DIM)
    k_nope, v = jnp.split(kv, [MLA_NOPE_DIM], axis=-1)
    k_rope = jnp.broadcast_to(krope[:, :, None, :], (B, S, MLA_HEADS, MLA_ROPE_DIM))
    k = jnp.concatenate([k_nope, k_rope], axis=-1)
    q = rms_norm(q, q_g)
    k = rms_norm(k, k_g)
    q = jnp.concatenate([q[..., :MLA_NOPE_DIM], apply_rope(q[..., MLA_NOPE_DIM:], cos, sin)], axis=-1)
    k = jnp.concatenate([k[..., :MLA_NOPE_DIM], apply_rope(k[..., MLA_NOPE_DIM:], cos, sin)], axis=-1)
    o = causal_attention(q, k, v)
    return o.reshape(B, S, MLA_HEADS * MLA_V_DIM)


def pool_mixer(u, w_pool, scale):
    B, S, _ = u.shape
    uf = u.astype(jnp.float32).reshape(B, S, len(POOL_WINDOWS), POOL_GROUP)
    cs = jnp.cumsum(uf, axis=1)
    t1 = jnp.arange(1, S + 1, dtype=jnp.float32)
    outs = []
    for gi, w in enumerate(POOL_WINDOWS):
        csg = cs[:, :, gi]
        lag = jnp.pad(csg, ((0, 0), (w, 0), (0, 0)))[:, :S]
        mean = (csg - lag) / jnp.minimum(t1, w)[None, :, None]
        outs.append(mean - uf[:, :, gi])
    pooled = jnp.stack(outs, axis=2).astype(u.dtype)
    y = jnp.einsum('bsgc,gcd->bsgd', pooled, w_pool)
    return y.reshape(B, S, GROUP_WIDTH) * scale


def gated_recurrence_chunked(q, k, v, logf):
    B, S, H, DK = q.shape
    DV = v.shape[-1]
    nc = S // HGRN_CHUNK

    def to_chunks(a):
        return a.reshape(B, nc, HGRN_CHUNK, H, a.shape[-1]).transpose(1, 0, 3, 2, 4)

    causal = jnp.tril(jnp.ones((HGRN_CHUNK, HGRN_CHUNK), dtype=bool))

    def step(state, inp):
        qc, kc, vc, lc = inp
        b = jnp.cumsum(lc, axis=2)
        o_inter = jnp.einsum('bhtd,bhdv->bhtv', qc * jnp.exp(b), state)
        diff = b[:, :, :, None, :] - b[:, :, None, :, :]
        decay = jnp.exp(jnp.where(causal[:, :, None], diff, -jnp.inf))
        a = jnp.einsum('bhtd,bhtsd,bhsd->bhts', qc, decay, kc)
        o_intra = jnp.einsum('bhts,bhsv->bhtv', a, vc)
        b_last = b[:, :, -1:, :]
        new_state = jnp.exp(b_last[:, :, 0, :])[..., None] * state + \
            jnp.einsum('bhsd,bhsv->bhdv', kc * jnp.exp(b_last - b), vc)
        return new_state, o_inter + o_intra

    state0 = jnp.zeros((B, H, DK, DV), jnp.float32)
    _, o = lax.scan(step, state0, (to_chunks(q), to_chunks(k), to_chunks(v), to_chunks(logf)))
    return o.transpose(1, 0, 3, 2, 4).reshape(B, S, H, DV)


def hgrn2_mixer(hq, hf, hi, hg, lb, out_g):
    B, S, _ = hq.shape
    q = jax.nn.silu(hq.astype(jnp.float32)).reshape(B, S, HGRN_HEADS, HGRN_DK)
    zf = hf.astype(jnp.float32).reshape(B, S, HGRN_HEADS, HGRN_DK)
    lbh = lb.reshape(HGRN_HEADS, HGRN_DK)
    logf = jnp.logaddexp(jnp.log(lbh), jnp.log1p(-lbh) + jax.nn.log_sigmoid(zf))
    k = -jnp.expm1(logf)
    v = hi.astype(jnp.float32).reshape(B, S, HGRN_HEADS, HGRN_DV)
    o = gated_recurrence_chunked(q, k, v, logf)
    g = hg.astype(jnp.float32).reshape(B, S, HGRN_HEADS, HGRN_DV)
    o = rms_norm(o, out_g) * jax.nn.silu(g)
    return o.reshape(B, S, GROUP_WIDTH).astype(hq.dtype)


def conformer_conv_mixer(u2, dw_w, dw_b, ln_g, ln_b):
    a, gate = jnp.split(u2, 2, axis=-1)
    u = a * jax.nn.sigmoid(gate)
    u = lax.conv_general_dilated(
        u, dw_w[:, None, :].astype(u.dtype), window_strides=(1,), padding=[(CONV_K - 1, 0)],
        dimension_numbers=('NWC', 'WIO', 'NWC'), feature_group_count=CONV_CH) + dw_b
    u = layer_norm(u, ln_g, ln_b)
    return jax.nn.silu(u)


def token_mixer(h, cos, sin, lb, w_in, q_a_g, w_uq, kv_a_g, w_ukv, q_g, k_g, w_pool, pool_scale,
                hgrn_out_g, dw_w, dw_b, ln_g, ln_b, w_out):
    z = h @ w_in
    cq, ckv, krope, u_pool, hq, hf, hi, hg, u_conv = jnp.split(z, list(IN_OFFSETS), axis=-1)
    y_mla = mla_mixer(cq, ckv, krope, cos, sin, q_a_g, w_uq, kv_a_g, w_ukv, q_g, k_g)
    y_pool = pool_mixer(u_pool, w_pool, pool_scale)
    y_hgrn = hgrn2_mixer(hq, hf, hi, hg, lb, hgrn_out_g)
    y_conv = conformer_conv_mixer(u_conv, dw_w, dw_b, ln_g, ln_b)
    y = jnp.concatenate([y_mla, y_pool, y_hgrn, y_conv], axis=-1)
    return y @ w_out


def hierarchical_moe(h, rg_w, rg_b, re_w, re_b, w1, w3, w2):
    B, S, D = h.shape
    N = B * S
    xf = h.reshape(N, D)
    g_logits = (xf @ rg_w).astype(jnp.float32) + rg_b
    g_prob = jax.nn.softmax(g_logits, axis=-1)
    g_idx = jnp.argmax(g_logits, axis=-1).astype(jnp.int32)
    g_w = jnp.take_along_axis(g_prob, g_idx[:, None], axis=-1)
    e_logits = ((xf @ re_w).astype(jnp.float32) + re_b).reshape(N, N_EXPERT_GROUPS, EXPERTS_PER_GROUP)
    e_logits = jnp.take_along_axis(e_logits, g_idx[:, None, None], axis=1)[:, 0]
    top_l, top_i = lax.top_k(e_logits, TOP_K)
    e_w = jax.nn.softmax(top_l, axis=-1) * g_w
    expert_id = (g_idx[:, None] * EXPERTS_PER_GROUP + top_i).reshape(-1).astype(jnp.int32)
    weight = e_w.reshape(-1)
    token = jnp.repeat(jnp.arange(N, dtype=jnp.int32), TOP_K)
    A = N * TOP_K
    order = jnp.argsort(expert_id)
    e_sorted = expert_id[order]
    tok_sorted = token[order]
    w_sorted = weight[order]
    counts = jax.ops.segment_sum(jnp.ones((A,), jnp.int32), expert_id, num_segments=N_EXPERTS)
    start = jnp.cumsum(counts) - counts
    padded = (counts + MOE_BLOCK - 1) // MOE_BLOCK * MOE_BLOCK
    padded_end = jnp.cumsum(padded)
    padded_start = padded_end - padded
    dest = padded_start[e_sorted] + jnp.arange(A, dtype=jnp.int32) - start[e_sorted]
    n_rows = (-(-A // MOE_BLOCK) + N_EXPERTS) * MOE_BLOCK
    n_blocks = n_rows // MOE_BLOCK
    row_tok = jnp.full((n_rows,), N, jnp.int32).at[dest].set(tok_sorted)
    row_w = jnp.zeros((n_rows,), jnp.float32).at[dest].set(w_sorted)
    blk_start = jnp.arange(n_blocks, dtype=jnp.int32) * MOE_BLOCK
    blk_expert = jnp.minimum(jnp.searchsorted(padded_end, blk_start, side='right'), N_EXPERTS - 1)
    x_pad = jnp.concatenate([xf, jnp.zeros((1, D), xf.dtype)], axis=0)

    def run_block(args):
        toks, wts, e = args
        xb = x_pad[toks]
        hb = jax.nn.silu(xb @ w1[e]) * (xb @ w3[e])
        return (hb @ w2[e]) * wts[:, None].astype(xb.dtype)

    yb = lax.map(run_block, (row_tok.reshape(n_blocks, MOE_BLOCK), row_w.reshape(n_blocks, MOE_BLOCK), blk_expert))
    y = jnp.zeros((N + 1, D), h.dtype).at[row_tok].add(yb.reshape(n_rows, D).astype(h.dtype))
    return y[:N].reshape(B, S, D)


def setup_inputs(seed: int = 0) -> dict:
    key = jax.random.key(seed)
    ks = jax.random.split(key, 32)
    f32 = jnp.float32
    L, D = DEPTH, D_MODEL

    def nrm(k, shape, scale):
        return jax.random.normal(k, shape, f32) * scale

    def gain(k, shape):
        return 1.0 + 0.02 * jax.random.normal(k, shape, f32)

    return {
        'x': nrm(ks[0], (BATCH, SEQ, D), 1.0),
        'c': nrm(ks[1], (BATCH, D), 1.0),
        'positions': jnp.arange(SEQ, dtype=jnp.int32)[None, :] + jax.random.randint(ks[2], (BATCH, 1), 0, 4096, dtype=jnp.int32),
        'w_ada': nrm(ks[3], (L, D, 6 * D), 0.5 * D ** -0.5),
        'b_ada': nrm(ks[4], (L, 6 * D), 0.02),
        'norm1_g': gain(ks[5], (L, D)),
        'norm2_g': gain(ks[6], (L, D)),
        'w_in': nrm(ks[7], (L, D, IN_COLS), D ** -0.5),
        'q_a_norm_g': gain(ks[8], (L, MLA_Q_LORA)),
        'w_uq': nrm(ks[9], (L, MLA_Q_LORA, MLA_HEADS * MLA_QK_DIM), MLA_Q_LORA ** -0.5),
        'kv_a_norm_g': gain(ks[10], (L, MLA_KV_LORA)),
        'w_ukv': nrm(ks[11], (L, MLA_KV_LORA, MLA_HEADS * (MLA_NOPE_DIM + MLA_V_DIM)), MLA_KV_LORA ** -0.5),
        'q_norm_g': gain(ks[12], (L, MLA_QK_DIM)),
        'k_norm_g': gain(ks[13], (L, MLA_QK_DIM)),
        'w_pool': nrm(ks[14], (L, len(POOL_WINDOWS), POOL_GROUP, POOL_GROUP), POOL_GROUP ** -0.5),
        'pool_scale': 1.0 + 0.1 * jax.random.normal(ks[15], (L, GROUP_WIDTH), f32),
        'hgrn_lb_logits': nrm(ks[16], (DEPTH, HGRN_HEADS * HGRN_DK), 1.0),
        'hgrn_out_norm_g': gain(ks[17], (L, HGRN_DV)),
        'conv_dw_w': nrm(ks[18], (L, CONV_K, CONV_CH), CONV_K ** -0.5),
        'conv_dw_b': nrm(ks[19], (L, CONV_CH), 0.02),
        'conv_ln_g': gain(ks[20], (L, CONV_CH)),
        'conv_ln_b': nrm(ks[21], (L, CONV_CH), 0.02),
        'w_out': nrm(ks[22], (L, D_MIX, D), D_MIX ** -0.5),
        'router_group_w': nrm(ks[23], (L, D, N_EXPERT_GROUPS), D ** -0.5),
        'router_group_b': nrm(ks[24], (L, N_EXPERT_GROUPS), 0.01),
        'router_expert_w': nrm(ks[25], (L, D, N_EXPERTS), D ** -0.5),
        'router_expert_b': nrm(ks[26], (L, N_EXPERTS), 0.01),
        'w1': nrm(ks[27], (L, N_EXPERTS, D, D_EXPERT), D ** -0.5),
        'w3': nrm(ks[28], (L, N_EXPERTS, D, D_EXPERT), D ** -0.5),
        'w2': nrm(ks[29], (L, N_EXPERTS, D_EXPERT, D), D_EXPERT ** -0.5),
    }


def reference(x, c, positions, w_ada, b_ada, norm1_g, norm2_g, w_in, q_a_norm_g, w_uq, kv_a_norm_g, w_ukv,
              q_norm_g, k_norm_g, w_pool, pool_scale, hgrn_lb_logits, hgrn_out_norm_g, conv_dw_w, conv_dw_b,
              conv_ln_g, conv_ln_b, w_out, router_group_w, router_group_b, router_expert_w, router_expert_b,
              w1, w3, w2):
    cos, sin = rope_tables(positions)
    lb_cum = jnp.cumsum(jax.nn.softmax(hgrn_lb_logits.astype(jnp.float32), axis=0), axis=0)
    lower_bounds = lb_cum - lb_cum[0:1]
    c_act = jax.nn.silu(c)
    for l in range(DEPTH):
        mod = (c_act @ w_ada[l] + b_ada[l])[:, None, :]
        sh1, sc1, g1, sh2, sc2, g2 = jnp.split(mod, 6, axis=-1)
        h = rms_norm(x, norm1_g[l]) * (1 + sc1) + sh1
        x = x + g1 * token_mixer(h, cos, sin, lower_bounds[l], w_in[l], q_a_norm_g[l], w_uq[l], kv_a_norm_g[l],
                                 w_ukv[l], q_norm_g[l], k_norm_g[l], w_pool[l], pool_scale[l], hgrn_out_norm_g[l],
                                 conv_dw_w[l], conv_dw_b[l], conv_ln_g[l], conv_ln_b[l], w_out[l])
        h = rms_norm(x, norm2_g[l]) * (1 + sc2) + sh2
        x = x + g2 * hierarchical_moe(h, router_group_w[l], router_group_b[l], router_expert_w[l],
                                      router_expert_b[l], w1[l], w3[l], w2[l])
    return x
```

```python
import functools
import math

import jax
import jax.numpy as jnp
from jax import lax
from jax.experimental import pallas as pl
from jax.experimental.pallas import tpu as pltpu

f32 = jnp.float32
bf16 = jnp.bfloat16
i32 = jnp.int32

D_MODEL = 1024
GROUP_WIDTH = 256
N_HEADS = 4
MLA_QK = 96
MLA_NOPE = 64
MLA_ROPE = 32
MLA_V = 64
MLA_Q_LORA = 192
MLA_KV_LORA = 128
HEAD_PAD = 128
HGRN_D = 64
SUB = 16
CONV_K = 31
POOL_WINDOWS = (2, 4, 8, 16)
N_GROUPS = 4
EXPERTS_PER_GROUP = 8
N_EXPERTS = 32
D_EXPERT = 512
MOE_BLOCK = 128
EPS = 1e-6
ROPE_THETA = 10000.0

Z_REST = 7 * GROUP_WIDTH
Z_COLS = Z_REST + 256 + 128 + 128
ROUTER_ROWS = 40

VMEM_LIMIT = 56 * 1024 * 1024


def _cparams(sem):
    return pltpu.CompilerParams(dimension_semantics=sem, vmem_limit_bytes=VMEM_LIMIT)


def _nt(a, b):
    return lax.dot_general(a, b, (((1,), (1,)), ((), ())), preferred_element_type=f32)


def _tn(a, b):
    return lax.dot_general(a, b, (((0,), (0,)), ((), ())), preferred_element_type=f32)


def _dot(a, b):
    return jnp.dot(a, b, preferred_element_type=f32)


def _split3(x):
    hi = x.astype(bf16)
    r1 = x - hi.astype(f32)
    mid = r1.astype(bf16)
    lo = (r1 - mid.astype(f32)).astype(bf16)
    return hi, mid, lo


def _silu(x):
    return x * jax.nn.sigmoid(x)


def _ada_kernel(c_ref, w_ref, b_ref, o_ref):
    c = c_ref[...]
    o_ref[0] = _dot(_silu(c).astype(bf16), w_ref[0].astype(bf16)) + b_ref[0]


def _ada_mod(c, w_ada, b_ada):
    L = w_ada.shape[0]
    B = c.shape[0]
    n6 = w_ada.shape[2]
    tn = 1536
    c8 = jnp.zeros((8, D_MODEL), f32).at[:B].set(c)
    out = pl.pallas_call(
        _ada_kernel,
        out_shape=jax.ShapeDtypeStruct((L, 8, n6), f32),
        grid=(L, n6 // tn),
        in_specs=[pl.BlockSpec((8, D_MODEL), lambda l, j: (0, 0)),
                  pl.BlockSpec((1, D_MODEL, tn), lambda l, j: (l, 0, j)),
                  pl.BlockSpec((1, 1, tn), lambda l, j: (l, 0, j))],
        out_specs=pl.BlockSpec((1, 8, tn), lambda l, j: (l, 0, j)),
        compiler_params=_cparams(("arbitrary", "arbitrary")),
        name="ada_mod",
    )(c8, w_ada, b_ada.reshape(L, 1, n6))
    return out[:, :B]


def _head_norm_rope(t, g, ctab, s1tab, s2tab):
    segs = []
    for h in range(N_HEADS):
        seg = t[:, h * HEAD_PAD:(h + 1) * HEAD_PAD]
        ms = jnp.sum(seg * seg, axis=-1, keepdims=True) * (1.0 / MLA_QK)
        segs.append(seg * lax.rsqrt(ms + EPS) * g)
    tn_ = jnp.concatenate(segs, axis=1)
    width = N_HEADS * HEAD_PAD
    half = MLA_ROPE // 2
    c4 = jnp.concatenate([ctab] * N_HEADS, axis=1)
    s14 = jnp.concatenate([s1tab] * N_HEADS, axis=1)
    s24 = jnp.concatenate([s2tab] * N_HEADS, axis=1)
    return tn_ * c4 + pltpu.roll(tn_, half, 1) * s14 + pltpu.roll(tn_, width - half, 1) * s24


def _mixer_in_kernel(x_ref, sh_ref, sc_ref, g1_ref, win_ref, qag_ref, wuq_ref, kvag_ref, wk_ref, wv_ref,
                     qg_ref, kg_ref, ctab_ref, s1_ref, s2_ref,
                     zrest_ref, q_ref, k_ref, v_ref):
    x = x_ref[...]
    ms = jnp.mean(x * x, axis=-1, keepdims=True)
    h = x * lax.rsqrt(ms + EPS) * g1_ref[...]
    h = h * (1.0 + sc_ref[0]) + sh_ref[0]
    z = _dot(h.astype(bf16), win_ref[...])
    zrest_ref[...] = z[:, :Z_REST]
    cq = z[:, Z_REST:Z_REST + 256]
    ckv = z[:, Z_REST + 256:Z_REST + 384]
    kr = z[:, Z_REST + 384:Z_REST + 512]
    ctab = ctab_ref[...]
    s1 = s1_ref[...]
    s2 = s2_ref[...]

    cqn = cq * lax.rsqrt(jnp.sum(cq * cq, axis=-1, keepdims=True) * (1.0 / MLA_Q_LORA) + EPS) * qag_ref[...]
    q = _dot(cqn.astype(bf16), wuq_ref[...])
    q = _head_norm_rope(q, qg_ref[...], ctab, s1, s2)
    q_ref[...] = (q * (MLA_QK ** -0.5 * math.log2(math.e))).astype(bf16)

    ckvn = ckv * lax.rsqrt(jnp.mean(ckv * ckv, axis=-1, keepdims=True) + EPS) * kvag_ref[...]
    ckvb = ckvn.astype(bf16)
    k = _dot(ckvb, wk_ref[...]) + jnp.concatenate([kr] * N_HEADS, axis=1)
    k = _head_norm_rope(k, kg_ref[...], ctab, s1, s2)
    k_ref[...] = k.astype(bf16)
    v_ref[...] = _dot(ckvb, wv_ref[...]).astype(bf16)


def _mixer_in(x, sh, sc, g1, win, qag, wuq, kvag, wk, wv, qg, kg, ctab, s1tab, s2tab, seq, tm):
    n = x.shape[0]
    tpb = seq // tm
    hw = N_HEADS * HEAD_PAD
    row = lambda i: (i, 0)
    fixed = lambda i: (0, 0)
    per_b = lambda i: (i // tpb, 0, 0)
    return pl.pallas_call(
        _mixer_in_kernel,
        out_shape=(jax.ShapeDtypeStruct((n, Z_REST), f32),
                   jax.ShapeDtypeStruct((n, hw), bf16),
                   jax.ShapeDtypeStruct((n, hw), bf16),
                   jax.ShapeDtypeStruct((n, hw), bf16)),
        grid=(n // tm,),
        in_specs=[pl.BlockSpec((tm, D_MODEL), row),
                  pl.BlockSpec((1, 1, D_MODEL), per_b),
                  pl.BlockSpec((1, 1, D_MODEL), per_b),
                  pl.BlockSpec((1, D_MODEL), fixed),
                  pl.BlockSpec((D_MODEL, Z_COLS), fixed),
                  pl.BlockSpec((1, 256), fixed),
                  pl.BlockSpec((256, hw), fixed),
                  pl.BlockSpec((1, MLA_KV_LORA), fixed),
                  pl.BlockSpec((MLA_KV_LORA, hw), fixed),
                  pl.BlockSpec((MLA_KV_LORA, hw), fixed),
                  pl.BlockSpec((1, HEAD_PAD), fixed),
                  pl.BlockSpec((1, HEAD_PAD), fixed),
                  pl.BlockSpec((tm, HEAD_PAD), row),
                  pl.BlockSpec((tm, HEAD_PAD), row),
                  pl.BlockSpec((tm, HEAD_PAD), row)],
        out_specs=(pl.BlockSpec((tm, Z_REST), row),
                   pl.BlockSpec((tm, hw), row),
                   pl.BlockSpec((tm, hw), row),
                   pl.BlockSpec((tm, hw), row)),
        compiler_params=_cparams(("arbitrary",)),
        name="mixer_in",
    )(x, sh, sc, g1, win, qag, wuq, kvag, wk, wv, qg, kg, ctab, s1tab, s2tab)


def _attn_kernel(q_ref, k_ref, v_ref, o_ref, m_scr, l_scr, acc_scr, *, tq):
    qi = pl.program_id(2)
    q = q_ref[0]
    m_scr[...] = jnp.full(m_scr.shape, -jnp.inf, f32)
    l_scr[...] = jnp.zeros(l_scr.shape, f32)
    acc_scr[...] = jnp.zeros(acc_scr.shape, f32)

    def step(j, masked):
        start = pl.multiple_of(j * tq, tq)
        kb = k_ref[0, pl.ds(start, tq), :]
        vb = v_ref[0, pl.ds(start, tq), :]
        s = _nt(q, kb)
        if masked:
            r = lax.broadcasted_iota(i32, (tq, tq), 0)
            c = lax.broadcasted_iota(i32, (tq, tq), 1)
            s = jnp.where(c <= r, s, -1e30)
        m_prev = m_scr[...]
        m_new = jnp.maximum(m_prev, jnp.max(s, axis=-1, keepdims=True))
        alpha = jnp.exp2(m_prev - m_new)
        p = jnp.exp2(s - m_new)
        l_scr[...] = alpha * l_scr[...] + jnp.sum(p, axis=-1, keepdims=True)
        acc_scr[...] = alpha * acc_scr[...] + _dot(p.astype(bf16), vb)
        m_scr[...] = m_new

    def body(j, carry):
        step(j, False)
        return carry

    lax.fori_loop(0, qi, body, 0)
    step(qi, True)
    o_ref[0] = (acc_scr[...] / l_scr[...]).astype(bf16)


def _attention(q, k, v, tq):
    b, s, hw = q.shape
    qspec = pl.BlockSpec((1, tq, HEAD_PAD), lambda bi, h, qi: (bi, qi, h))
    kvspec = pl.BlockSpec((1, s, HEAD_PAD), lambda bi, h, qi: (bi, 0, h))
    return pl.pallas_call(
        functools.partial(_attn_kernel, tq=tq),
        out_shape=jax.ShapeDtypeStruct((b, s, hw), bf16),
        grid=(b, N_HEADS, s // tq),
        in_specs=[qspec, kvspec, kvspec],
        out_specs=qspec,
        scratch_shapes=[pltpu.VMEM((tq, 1), f32), pltpu.VMEM((tq, 1), f32), pltpu.VMEM((tq, HEAD_PAD), f32)],
        compiler_params=_cparams(("arbitrary", "arbitrary", "arbitrary")),
        name="mla_attention",
    )(q, k, v)


def _seqmix_kernel(z_ref, wpool_ref, pscale_ref, loglb_ref, l1mlb_ref, oml_ref, og_ref, dww_ref, dwb_ref,
                   lng_ref, lnb_ref, le_ref, bones_ref, sel_ref, bdmask_ref, cmask_ref,
                   y_ref, pool_ext, conv_ext, st_scr, *, t):
    step = pl.program_id(1)
    gw = GROUP_WIDTH

    @pl.when(step == 0)
    def _():
        pool_ext[0:16, :] = jnp.zeros((16, gw), f32)
        conv_ext[0:32, :] = jnp.zeros((32, gw), f32)
        st_scr[...] = jnp.zeros(st_scr.shape, f32)

    u = z_ref[:, 0:gw]
    pool_ext[16:16 + t, :] = u
    sums = {}
    acc = u
    for back in range(1, 16):
        acc = acc + pool_ext[pl.ds(16 - back, t), :]
        if back + 1 in POOL_WINDOWS:
            sums[back + 1] = acc
    col = lax.broadcasted_iota(i32, (t, gw), 1)
    pos1 = (step * t + lax.broadcasted_iota(i32, (t, gw), 0) + 1).astype(f32)
    wsum = jnp.where(col < 64, sums[2], jnp.where(col < 128, sums[4], jnp.where(col < 192, sums[8], sums[16])))
    wlen = jnp.where(col < 64, 2.0, jnp.where(col < 128, 4.0, jnp.where(col < 192, 8.0, 16.0)))
    pooled = wsum / jnp.minimum(pos1, wlen) - u
    y_ref[:, 0:gw] = (_dot(pooled.astype(bf16), wpool_ref[...]) * pscale_ref[...]).astype(bf16)
    pool_ext[0:16, :] = pool_ext[t:t + 16, :]

    uc = z_ref[:, 5 * gw:6 * gw] * jax.nn.sigmoid(z_ref[:, 6 * gw:7 * gw])
    conv_ext[32:32 + t, :] = uc
    cacc = jnp.zeros((t, gw), f32) + dwb_ref[...]
    for j in range(CONV_K):
        cacc = cacc + conv_ext[pl.ds(32 - (CONV_K - 1) + j, t), :] * dww_ref[j:j + 1, :]
    mu = jnp.mean(cacc, axis=-1, keepdims=True)
    cen = cacc - mu
    var = jnp.mean(cen * cen, axis=-1, keepdims=True)
    cn = cen * lax.rsqrt(var + EPS) * lng_ref[...] + lnb_ref[...]
    y_ref[:, 2 * gw:3 * gw] = _silu(cn).astype(bf16)
    conv_ext[0:32, :] = conv_ext[t:t + 32, :]

    hq = z_ref[:, 1 * gw:2 * gw]
    hf = z_ref[:, 2 * gw:3 * gw]
    v = z_ref[:, 3 * gw:4 * gw]
    hg = z_ref[:, 4 * gw:5 * gw]
    q = _silu(hq)
    ls = jnp.minimum(hf, 0.0) - jnp.log1p(jnp.exp(-jnp.abs(hf)))
    x1 = loglb_ref[...]
    x2 = l1mlb_ref[...] + ls
    logf = jnp.maximum(x1, x2) + jnp.log1p(jnp.exp(-jnp.abs(x1 - x2)))
    kk = oml_ref[...] * jax.nn.sigmoid(-hf)
    le = le_ref[...]
    hi, mid, lo = _split3(logf)
    cs = _dot(le, hi) + _dot(le, mid) + _dot(le, lo)
    bl = cs[0:t]
    be = cs[t:2 * t]
    qt = (q * jnp.exp(bl)).astype(bf16)
    kt = (kk * jnp.exp(be - bl)).astype(bf16)
    dec = jnp.exp(be)
    vb = v.astype(bf16)
    bones = bones_ref[...]
    sel = sel_ref[...]
    bdmask = bdmask_ref[...]
    cmask = cmask_ref[...] > 0.0
    sq = SUB * SUB

    def rep_t(a):
        return jnp.broadcast_to(a[:, None, :], (SUB, SUB, gw)).reshape(sq, gw)

    def rep_s(a):
        return jnp.broadcast_to(a[None, :, :], (SUB, SUB, gw)).reshape(sq, gw)

    st = st_scr[...]
    outs = []
    for j in range(t // SUB):
        rows = slice(j * SUB, (j + 1) * SUB)
        o_inter = _nt(qt[rows], st.astype(bf16))
        diff = rep_t(bl[rows]) - rep_s(bl[rows])
        e = jnp.exp(jnp.where(cmask, diff, -1e30))
        p = (rep_t(q[rows]) * rep_s(kk[rows]) * e).astype(bf16)
        a = _dot(p, bones)
        w = (a * rep_s(v[rows])).astype(bf16)
        o_diag = _dot(sel, w)
        outs.append(o_inter + o_diag)
        st = st * dec[j * SUB:j * SUB + 1, :] + bdmask * _tn(vb[rows], kt[rows])
    st_scr[...] = st
    o = jnp.concatenate(outs, axis=0)
    o2 = o * o
    ohi = o2.astype(bf16)
    olo = (o2 - ohi.astype(f32)).astype(bf16)
    msq = (_dot(ohi, bones) + _dot(olo, bones)) * (1.0 / HGRN_D)
    on = o * lax.rsqrt(msq + EPS) * og_ref[...]
    y_ref[:, gw:2 * gw] = (on * _silu(hg)).astype(bf16)


def _seqmix_consts(t):
    r = jnp.arange(t)
    same = (r[:, None] // SUB) == (r[None, :] // SUB)
    ltri = same & (r[None, :] <= r[:, None])
    le = jnp.concatenate([ltri, same], axis=0).astype(bf16)
    c = jnp.arange(GROUP_WIDTH)
    bd = (c[:, None] // HGRN_D) == (c[None, :] // HGRN_D)
    bones = bd.astype(bf16)
    bdmask = bd.astype(f32)
    p = jnp.arange(SUB * SUB)
    sel = (p[None, :] // SUB == jnp.arange(SUB)[:, None]).astype(bf16)
    cmask = ((p % SUB) <= (p // SUB)).astype(f32)[:, None] * jnp.ones((1, GROUP_WIDTH), f32)
    return le, bones, sel, bdmask, cmask


def _seqmix(zrest, wpool, pscale, loglb, l1mlb, oml, og, dww, dwb, lng, lnb, consts, batch, seq, t):
    n = zrest.shape[0]
    spb = seq // t
    le, bones, sel, bdmask, cmask = consts
    row = lambda b, s: (b * spb + s, 0)
    fixed = lambda b, s: (0, 0)
    vec = pl.BlockSpec((1, GROUP_WIDTH), fixed)
    return pl.pallas_call(
        functools.partial(_seqmix_kernel, t=t),
        out_shape=jax.ShapeDtypeStruct((n, 3 * GROUP_WIDTH), bf16),
        grid=(batch, spb),
        in_specs=[pl.BlockSpec((t, Z_REST), row),
                  pl.BlockSpec((GROUP_WIDTH, GROUP_WIDTH), fixed),
                  vec, vec, vec, vec, vec,
                  pl.BlockSpec((32, GROUP_WIDTH), fixed),
                  vec, vec, vec,
                  pl.BlockSpec((2 * t, t), fixed),
                  pl.BlockSpec((GROUP_WIDTH, GROUP_WIDTH), fixed),
                  pl.BlockSpec((SUB, SUB * SUB), fixed),
                  pl.BlockSpec((GROUP_WIDTH, GROUP_WIDTH), fixed),
                  pl.BlockSpec((SUB * SUB, GROUP_WIDTH), fixed)],
        out_specs=pl.BlockSpec((t, 3 * GROUP_WIDTH), row),
        scratch_shapes=[pltpu.VMEM((t + 16, GROUP_WIDTH), f32),
                        pltpu.VMEM((t + 32, GROUP_WIDTH), f32),
                        pltpu.VMEM((GROUP_WIDTH, GROUP_WIDTH), f32)],
        compiler_params=_cparams(("arbitrary", "arbitrary")),
        name="seq_mixers",
    )(zrest, wpool, pscale, loglb, l1mlb, oml, og, dww, dwb, lng, lnb, le, bones, sel, bdmask, cmask)


def _mixer_out_kernel(o_ref, y_ref, x_ref, gate_ref, wout_ref, n2g_ref, sc_ref, sh_ref, wr_ref, rb_ref,
                      x1_ref, h2_ref, eid_ref, ew_ref):
    hw = N_HEADS * HEAD_PAD
    tmix = _dot(o_ref[...], wout_ref[0:hw, :]) + _dot(y_ref[...], wout_ref[hw:, :])
    x1 = x_ref[...] + gate_ref[0] * tmix
    x1_ref[...] = x1
    ms = jnp.mean(x1 * x1, axis=-1, keepdims=True)
    h2 = x1 * lax.rsqrt(ms + EPS) * n2g_ref[...]
    h2 = h2 * (1.0 + sc_ref[0]) + sh_ref[0]
    h2_ref[...] = h2
    hhi = h2.astype(bf16)
    hlo = (h2 - hhi.astype(f32)).astype(bf16)
    wr = wr_ref[...]
    whi = wr.astype(bf16)
    wlo = (wr - whi.astype(f32)).astype(bf16)
    lg = _nt(whi, hhi) + _nt(whi, hlo) + _nt(wlo, hhi) + rb_ref[...]
    tm = lg.shape[1]
    g = [lg[i:i + 1, :] for i in range(N_GROUPS)]
    gmax = jnp.maximum(jnp.maximum(g[0], g[1]), jnp.maximum(g[2], g[3]))
    gidx = jnp.where(g[0] == gmax, 0, jnp.where(g[1] == gmax, 1, jnp.where(g[2] == gmax, 2, 3)))
    gsum = jnp.exp(g[0] - gmax) + jnp.exp(g[1] - gmax) + jnp.exp(g[2] - gmax) + jnp.exp(g[3] - gmax)
    gw_ = 1.0 / gsum
    e = [lg[8 + EXPERTS_PER_GROUP * i:8 + EXPERTS_PER_GROUP * (i + 1), :] for i in range(N_GROUPS)]
    esel = jnp.where(gidx == 0, e[0], jnp.where(gidx == 1, e[1], jnp.where(gidx == 2, e[2], e[3])))
    ri = lax.broadcasted_iota(i32, (EXPERTS_PER_GROUP, tm), 0)
    top1 = jnp.max(esel, axis=0, keepdims=True)
    idx1 = jnp.min(jnp.where(esel == top1, ri, EXPERTS_PER_GROUP), axis=0, keepdims=True)
    rest = jnp.where(ri == idx1, -jnp.inf, esel)
    top2 = jnp.max(rest, axis=0, keepdims=True)
    idx2 = jnp.min(jnp.where(rest == top2, ri, EXPERTS_PER_GROUP), axis=0, keepdims=True)
    e2 = jnp.exp(top2 - top1)
    den = 1.0 + e2
    zi = jnp.zeros((6, tm), i32)
    zf = jnp.zeros((6, tm), f32)
    eid_ref[...] = jnp.concatenate([gidx * EXPERTS_PER_GROUP + idx1, gidx * EXPERTS_PER_GROUP + idx2, zi], axis=0)
    ew_ref[...] = jnp.concatenate([(1.0 / den) * gw_, (e2 / den) * gw_, zf], axis=0)


def _mixer_out(o, y, x, gate, wout, n2g, sc, sh, wr, rb, seq, tm):
    n = x.shape[0]
    tpb = seq // tm
    hw = N_HEADS * HEAD_PAD
    row = lambda i: (i, 0)
    fixed = lambda i: (0, 0)
    per_b = lambda i: (i // tpb, 0, 0)
    colblk = lambda i: (0, i)
    return pl.pallas_call(
        _mixer_out_kernel,
        out_shape=(jax.ShapeDtypeStruct((n, D_MODEL), f32),
                   jax.ShapeDtypeStruct((n, D_MODEL), f32),
                   jax.ShapeDtypeStruct((8, n), i32),
                   jax.ShapeDtypeStruct((8, n), f32)),
        grid=(n // tm,),
        in_specs=[pl.BlockSpec((tm, hw), row),
                  pl.BlockSpec((tm, 3 * GROUP_WIDTH), row),
                  pl.BlockSpec((tm, D_MODEL), row),
                  pl.BlockSpec((1, 1, D_MODEL), per_b),
                  pl.BlockSpec((hw + 3 * GROUP_WIDTH, D_MODEL), fixed),
                  pl.BlockSpec((1, D_MODEL), fixed),
                  pl.BlockSpec((1, 1, D_MODEL), per_b),
                  pl.BlockSpec((1, 1, D_MODEL), per_b),
                  pl.BlockSpec((ROUTER_ROWS, D_MODEL), fixed),
                  pl.BlockSpec((ROUTER_ROWS, 1), fixed)],
        out_specs=(pl.BlockSpec((tm, D_MODEL), row),
                   pl.BlockSpec((tm, D_MODEL), row),
                   pl.BlockSpec((8, tm), colblk),
                   pl.BlockSpec((8, tm), colblk)),
        compiler_params=_cparams(("arbitrary",)),
        name="mixer_out_router",
    )(o, y, x, gate, wout, n2g, sc, sh, wr, rb)


def _row_copy(src_hbm, dst_hbm, src_row, dst_row, sem):
    return pltpu.make_async_copy(src_hbm.at[pl.ds(src_row, 1), :], dst_hbm.at[pl.ds(dst_row, 1), :], sem)


def _moe_kernel(be_ref, src_cur, src_nxt, dst_cur, dst_prev1, dst_prev2, roww_ref, w1_ref, w3_ref, w2_ref, h_hbm,
                y_hbm, xbuf, obuf, gsem, ssem):
    i = pl.program_id(0)
    nb = pl.num_programs(0)
    slot = i % 2

    def gather_copy(idx_ref, s, r):
        return pltpu.make_async_copy(h_hbm.at[pl.ds(idx_ref[0, 0, r], 1), :], xbuf.at[s, pl.ds(r, 1), :], gsem.at[s])

    def scatter_copy(idx_ref, s, r):
        return pltpu.make_async_copy(obuf.at[s, pl.ds(r, 1), :], y_hbm.at[pl.ds(idx_ref[0, 0, r], 1), :], ssem.at[s])

    def for_rows(fn):
        def body(r, c):
            fn(r)
            return c
        lax.fori_loop(0, MOE_BLOCK, body, 0, unroll=8)

    @pl.when(i == 0)
    def _():
        for_rows(lambda r: gather_copy(src_cur, 0, r).start())

    @pl.when(i + 1 < nb)
    def _():
        for_rows(lambda r: gather_copy(src_nxt, 1 - slot, r).start())

    for_rows(lambda r: gather_copy(src_cur, slot, r).wait())

    @pl.when(i >= 2)
    def _():
        for_rows(lambda r: scatter_copy(dst_prev2, slot, r).wait())

    xb = xbuf[slot].astype(bf16)
    a = _dot(xb, w1_ref[0, 0].astype(bf16))
    b = _dot(xb, w3_ref[0, 0].astype(bf16))
    hb = (_silu(a) * b).astype(bf16)
    obuf[slot] = _dot(hb, w2_ref[0, 0].astype(bf16)) * roww_ref[...]

    for_rows(lambda r: scatter_copy(dst_cur, slot, r).start())

    @pl.when(i == nb - 1)
    def _():
        for_rows(lambda r: scatter_copy(dst_cur, slot, r).wait())

        @pl.when(nb >= 2)
        def _():
            for_rows(lambda r: scatter_copy(dst_prev1, 1 - slot, r).wait())


def _moe_ffn(h2, blk_expert, row_src, row_dst, row_w, w1, w3, w2, layer):
    n_rows = row_src.shape[0]
    nb = n_rows // MOE_BLOCK
    src3 = row_src.reshape(nb, 1, MOE_BLOCK)
    dst3 = row_dst.reshape(nb, 1, MOE_BLOCK)
    smem_blk = lambda f: pl.BlockSpec((1, 1, MOE_BLOCK), f, memory_space=pltpu.SMEM)
    grid_spec = pltpu.PrefetchScalarGridSpec(
        num_scalar_prefetch=1,
        grid=(nb,),
        in_specs=[smem_blk(lambda i, be: (i, 0, 0)),
                  smem_blk(lambda i, be: (jnp.minimum(i + 1, nb - 1), 0, 0)),
                  smem_blk(lambda i, be: (i, 0, 0)),
                  smem_blk(lambda i, be: (jnp.maximum(i - 1, 0), 0, 0)),
                  smem_blk(lambda i, be: (jnp.maximum(i - 2, 0), 0, 0)),
                  pl.BlockSpec((MOE_BLOCK, 1), lambda i, be: (i, 0)),
                  pl.BlockSpec((1, 1, D_MODEL, D_EXPERT), lambda i, be: (layer, be[i], 0, 0)),
                  pl.BlockSpec((1, 1, D_MODEL, D_EXPERT), lambda i, be: (layer, be[i], 0, 0)),
                  pl.BlockSpec((1, 1, D_EXPERT, D_MODEL), lambda i, be: (layer, be[i], 0, 0)),
                  pl.BlockSpec(memory_space=pl.ANY)],
        out_specs=pl.BlockSpec(memory_space=pl.ANY),
        scratch_shapes=[pltpu.VMEM((2, MOE_BLOCK, D_MODEL), f32),
                        pltpu.VMEM((2, MOE_BLOCK, D_MODEL), f32),
                        pltpu.SemaphoreType.DMA((2,)),
                        pltpu.SemaphoreType.DMA((2,))])
    return pl.pallas_call(
        _moe_kernel,
        out_shape=jax.ShapeDtypeStruct((n_rows, D_MODEL), f32),
        grid_spec=grid_spec,
        compiler_params=_cparams(("arbitrary",)),
        name="moe_ffn",
    )(blk_expert, src3, src3, dst3, dst3, dst3, row_w, w1, w3, w2, h2)


def _moe_plan(eid, ew, n):
    a_total = 2 * n
    n_rows = a_total + N_EXPERTS * MOE_BLOCK
    nb = n_rows // MOE_BLOCK
    expert_id = eid[0:2].T.reshape(-1)
    weight = ew[0:2].T.reshape(-1)
    order = jnp.argsort(expert_id).astype(i32)
    counts = jnp.sum(expert_id[:, None] == jnp.arange(N_EXPERTS, dtype=i32)[None, :], axis=0, dtype=i32)
    start = jnp.cumsum(counts) - counts
    padded = (counts + MOE_BLOCK - 1) // MOE_BLOCK * MOE_BLOCK
    padded_end = jnp.cumsum(padded)
    padded_start = padded_end - padded
    blk_start = jnp.arange(nb, dtype=i32) * MOE_BLOCK
    blk_expert = jnp.minimum(jnp.searchsorted(padded_end, blk_start, side='right'), N_EXPERTS - 1).astype(i32)
    r = jnp.arange(n_rows, dtype=i32)
    e_r = jnp.repeat(blk_expert, MOE_BLOCK)
    off = r - padded_start[e_r]
    valid = (off >= 0) & (off < counts[e_r])
    a_r = order[jnp.clip(start[e_r] + off, 0, a_total - 1)]
    pad_rank = jnp.cumsum((~valid).astype(i32)) - 1
    row_src = jnp.where(valid, a_r // 2, 0).astype(i32)
    row_dst = jnp.where(valid, a_r, a_total + pad_rank).astype(i32)
    row_w = jnp.where(valid, weight[a_r], 0.0).astype(f32)[:, None]
    return blk_expert, row_src, row_dst, row_w


def _combine_kernel(x_ref, y_ref, gate_ref, o_ref):
    y = y_ref[...]
    o_ref[...] = x_ref[...] + gate_ref[0] * (y[:, :D_MODEL] + y[:, D_MODEL:])


def _combine(x1, y, gate, seq, tm):
    n = x1.shape[0]
    tpb = seq // tm
    y2 = y.reshape(y.shape[0] // 2, 2 * D_MODEL)
    return pl.pallas_call(
        _combine_kernel,
        out_shape=jax.ShapeDtypeStruct((n, D_MODEL), f32),
        grid=(n // tm,),
        in_specs=[pl.BlockSpec((tm, D_MODEL), lambda i: (i, 0)),
                  pl.BlockSpec((tm, 2 * D_MODEL), lambda i: (i, 0)),
                  pl.BlockSpec((1, 1, D_MODEL), lambda i: (i // tpb, 0, 0))],
        out_specs=pl.BlockSpec((tm, D_MODEL), lambda i: (i, 0)),
        compiler_params=_cparams(("arbitrary",)),
        name="moe_combine",
    )(x1, y2, gate)


def _pad_to(a, axis, size):
    pad = [(0, 0)] * a.ndim
    pad[axis] = (0, size - a.shape[axis])
    return jnp.pad(a, pad)


def _prep_layer_params(w_in, q_a_norm_g, w_uq, kv_a_norm_g, w_ukv, q_norm_g, k_norm_g, w_pool, pool_scale,
                       hgrn_lb_logits, hgrn_out_norm_g, conv_dw_w, w_out, router_group_w, router_group_b,
                       router_expert_w, router_expert_b):
    L = w_in.shape[0]
    zc = lambda n_: jnp.zeros((L, D_MODEL, n_), f32)
    win = jnp.concatenate([w_in[:, :, 352:], w_in[:, :, 0:192], zc(64), w_in[:, :, 192:320], zc(64),
                           w_in[:, :, 320:352], zc(32)], axis=2).astype(bf16)
    qag = _pad_to(q_a_norm_g, 1, 256)[:, None, :]
    wuq = _pad_to(_pad_to(w_uq.reshape(L, MLA_Q_LORA, N_HEADS, MLA_QK), 3, HEAD_PAD), 1, 256)
    wuq = wuq.reshape(L, 256, N_HEADS * HEAD_PAD).astype(bf16)
    kvag = kv_a_norm_g[:, None, :]
    wkv = w_ukv.reshape(L, MLA_KV_LORA, N_HEADS, MLA_NOPE + MLA_V)
    wk = _pad_to(wkv[..., :MLA_NOPE], 3, HEAD_PAD).reshape(L, MLA_KV_LORA, N_HEADS * HEAD_PAD).astype(bf16)
    wv = _pad_to(wkv[..., MLA_NOPE:], 3, HEAD_PAD).reshape(L, MLA_KV_LORA, N_HEADS * HEAD_PAD).astype(bf16)
    qg = _pad_to(q_norm_g, 1, HEAD_PAD)[:, None, :]
    kg = _pad_to(k_norm_g, 1, HEAD_PAD)[:, None, :]
    wpool = jnp.zeros((L, GROUP_WIDTH, GROUP_WIDTH), f32)
    for g in range(len(POOL_WINDOWS)):
        wpool = wpool.at[:, 64 * g:64 * (g + 1), 64 * g:64 * (g + 1)].set(w_pool[:, g])
    wpool = wpool.astype(bf16)
    lb_cum = jnp.cumsum(jax.nn.softmax(hgrn_lb_logits.astype(f32), axis=0), axis=0)
    lb = lb_cum - lb_cum[0:1]
    loglb = jnp.log(lb)[:, None, :]
    l1mlb = jnp.log1p(-lb)[:, None, :]
    oml = (1.0 - lb)[:, None, :]
    og = jnp.tile(hgrn_out_norm_g, (1, N_HEADS))[:, None, :]
    dww = _pad_to(conv_dw_w, 1, 32)
    wo_attn = _pad_to(w_out[:, :GROUP_WIDTH].reshape(L, N_HEADS, MLA_V, D_MODEL), 2, HEAD_PAD)
    wout = jnp.concatenate([wo_attn.reshape(L, N_HEADS * HEAD_PAD, D_MODEL), w_out[:, GROUP_WIDTH:]], axis=1).astype(bf16)
    wr = jnp.concatenate([jnp.swapaxes(router_group_w, 1, 2), jnp.zeros((L, 4, D_MODEL), f32),
                          jnp.swapaxes(router_expert_w, 1, 2)], axis=1)
    rb = jnp.concatenate([router_group_b, jnp.zeros((L, 4), f32), router_expert_b], axis=1)[:, :, None]
    return dict(win=win, qag=qag, wuq=wuq, kvag=kvag, wk=wk, wv=wv, qg=qg, kg=kg, wpool=wpool,
                pscale=pool_scale[:, None, :], loglb=loglb, l1mlb=l1mlb, oml=oml, og=og, dww=dww,
                wout=wout, wr=wr, rb=rb)


def _rope_tabs(positions):
    half = MLA_ROPE // 2
    inv_freq = ROPE_THETA ** (-jnp.arange(half, dtype=f32) / half)
    ang = positions.astype(f32).reshape(-1)[:, None] * inv_freq
    cos = jnp.cos(ang)
    sin = jnp.sin(ang)
    n = cos.shape[0]
    ctab = jnp.concatenate([jnp.ones((n, MLA_NOPE), f32), cos, cos, jnp.zeros((n, 32), f32)], axis=1)
    s1 = jnp.concatenate([jnp.zeros((n, MLA_NOPE + half), f32), sin, jnp.zeros((n, 32), f32)], axis=1)
    s2 = jnp.concatenate([jnp.zeros((n, MLA_NOPE), f32), -sin, jnp.zeros((n, half + 32), f32)], axis=1)
    return ctab, s1, s2


def kernel(x, c, positions, w_ada, b_ada, norm1_g, norm2_g, w_in, q_a_norm_g, w_uq, kv_a_norm_g, w_ukv, q_norm_g, k_norm_g, w_pool, pool_scale, hgrn_lb_logits, hgrn_out_norm_g, conv_dw_w, conv_dw_b, conv_ln_g, conv_ln_b, w_out, router_group_w, router_group_b, router_expert_w, router_expert_b, w1, w3, w2):
    B, S, D = x.shape
    L = w_in.shape[0]
    n = B * S
    tm = min(256, S)
    tq = min(512, S)
    tseq = min(128, S)

    p = _prep_layer_params(w_in, q_a_norm_g, w_uq, kv_a_norm_g, w_ukv, q_norm_g, k_norm_g, w_pool, pool_scale,
                           hgrn_lb_logits, hgrn_out_norm_g, conv_dw_w, w_out, router_group_w, router_group_b,
                           router_expert_w, router_expert_b)
    ctab, s1tab, s2tab = _rope_tabs(positions)
    consts = _seqmix_consts(tseq)
    mod = _ada_mod(c, w_ada, b_ada)
    xf = x.reshape(n, D)
    hw = N_HEADS * HEAD_PAD
    for l in range(L):
        m6 = [mod[l, :, i * D:(i + 1) * D][:, None, :] for i in range(6)]
        sh1, sc1, g1, sh2, sc2, g2 = m6
        zrest, q, k, v = _mixer_in(xf, sh1, sc1, norm1_g[l][None, :], p['win'][l], p['qag'][l], p['wuq'][l],
                                   p['kvag'][l], p['wk'][l], p['wv'][l], p['qg'][l], p['kg'][l],
                                   ctab, s1tab, s2tab, S, tm)
        o = _attention(q.reshape(B, S, hw), k.reshape(B, S, hw), v.reshape(B, S, hw), tq).reshape(n, hw)
        y = _seqmix(zrest, p['wpool'][l], p['pscale'][l], p['loglb'][l], p['l1mlb'][l], p['oml'][l], p['og'][l],
                    p['dww'][l], conv_dw_b[l][None, :], conv_ln_g[l][None, :], conv_ln_b[l][None, :],
                    consts, B, S, tseq)
        x1, h2, eid, ew = _mixer_out(o, y, xf, g1, p['wout'][l], norm2_g[l][None, :], sc2, sh2,
                                     p['wr'][l], p['rb'][l], S, tm)
        blk_expert, row_src, row_dst, row_w = _moe_plan(eid, ew, n)
        ymoe = _moe_ffn(h2, blk_expert, row_src, row_dst, row_w, w1, w3, w2, l)
        xf = _combine(x1, ymoe, g2, S, tm)
    return xf.reshape(B, S, D)
```

```python
import functools
import math

import jax
import jax.numpy as jnp
from jax import lax
from jax.experimental import pallas as pl
from jax.experimental.pallas import tpu as pltpu

f32 = jnp.float32
bf16 = jnp.bfloat16
i32 = jnp.int32

D_MODEL = 1024
GROUP_WIDTH = 256
N_HEADS = 4
MLA_QK = 96
MLA_NOPE = 64
MLA_ROPE = 32
MLA_V = 64
MLA_Q_LORA = 192
MLA_KV_LORA = 128
HEAD_PAD = 128
HGRN_D = 64
SUB = 16
CONV_K = 31
POOL_WINDOWS = (2, 4, 8, 16)
N_GROUPS = 4
EXPERTS_PER_GROUP = 8
N_EXPERTS = 32
D_EXPERT = 512
MOE_BLOCK = 256
EPS = 1e-6
ROPE_THETA = 10000.0

Z_REST = 7 * GROUP_WIDTH
Z_COLS = Z_REST + 256 + 128 + 128
ROUTER_ROWS = 40

VMEM_LIMIT = 56 * 1024 * 1024


def _cparams(sem):
    return pltpu.CompilerParams(dimension_semantics=sem, vmem_limit_bytes=VMEM_LIMIT)


def _nt(a, b):
    return lax.dot_general(a, b, (((1,), (1,)), ((), ())), preferred_element_type=f32)


def _tn(a, b):
    return lax.dot_general(a, b, (((0,), (0,)), ((), ())), preferred_element_type=f32)


def _dot(a, b):
    return jnp.dot(a, b, preferred_element_type=f32)


def _split3(x):
    hi = x.astype(bf16)
    r1 = x - hi.astype(f32)
    mid = r1.astype(bf16)
    lo = (r1 - mid.astype(f32)).astype(bf16)
    return hi, mid, lo


def _silu(x):
    return x * jax.nn.sigmoid(x)


def _ada_kernel(c_ref, w_ref, b_ref, o_ref):
    c = c_ref[...]
    o_ref[0] = _dot(_silu(c).astype(bf16), w_ref[0].astype(bf16)) + b_ref[0]


def _ada_mod(c, w_ada, b_ada):
    L = w_ada.shape[0]
    B = c.shape[0]
    n6 = w_ada.shape[2]
    tn = 1536
    c8 = jnp.zeros((8, D_MODEL), f32).at[:B].set(c)
    out = pl.pallas_call(
        _ada_kernel,
        out_shape=jax.ShapeDtypeStruct((L, 8, n6), f32),
        grid=(L, n6 // tn),
        in_specs=[pl.BlockSpec((8, D_MODEL), lambda l, j: (0, 0)),
                  pl.BlockSpec((1, D_MODEL, tn), lambda l, j: (l, 0, j)),
                  pl.BlockSpec((1, 1, tn), lambda l, j: (l, 0, j))],
        out_specs=pl.BlockSpec((1, 8, tn), lambda l, j: (l, 0, j)),
        compiler_params=_cparams(("arbitrary", "arbitrary")),
        name="ada_mod",
    )(c8, w_ada, b_ada.reshape(L, 1, n6))
    return out[:, :B]


def _head_norm_rope(t, g, ctab, s1tab, s2tab):
    segs = []
    for h in range(N_HEADS):
        seg = t[:, h * HEAD_PAD:(h + 1) * HEAD_PAD]
        ms = jnp.sum(seg * seg, axis=-1, keepdims=True) * (1.0 / MLA_QK)
        segs.append(seg * lax.rsqrt(ms + EPS) * g)
    tn_ = jnp.concatenate(segs, axis=1)
    width = N_HEADS * HEAD_PAD
    half = MLA_ROPE // 2
    c4 = jnp.concatenate([ctab] * N_HEADS, axis=1)
    s14 = jnp.concatenate([s1tab] * N_HEADS, axis=1)
    s24 = jnp.concatenate([s2tab] * N_HEADS, axis=1)
    return tn_ * c4 + pltpu.roll(tn_, half, 1) * s14 + pltpu.roll(tn_, width - half, 1) * s24


def _mixer_in_kernel(x_ref, sh_ref, sc_ref, g1_ref, win_ref, qag_ref, wuq_ref, kvag_ref, wk_ref, wv_ref,
                     qg_ref, kg_ref, ctab_ref, s1_ref, s2_ref,
                     zrest_ref, q_ref, k_ref, v_ref):
    x = x_ref[...]
    ms = jnp.mean(x * x, axis=-1, keepdims=True)
    h = x * lax.rsqrt(ms + EPS) * g1_ref[...]
    h = h * (1.0 + sc_ref[0]) + sh_ref[0]
    z = _dot(h.astype(bf16), win_ref[...])
    zrest_ref[...] = z[:, :Z_REST]
    cq = z[:, Z_REST:Z_REST + 256]
    ckv = z[:, Z_REST + 256:Z_REST + 384]
    kr = z[:, Z_REST + 384:Z_REST + 512]
    ctab = ctab_ref[...]
    s1 = s1_ref[...]
    s2 = s2_ref[...]

    cqn = cq * lax.rsqrt(jnp.sum(cq * cq, axis=-1, keepdims=True) * (1.0 / MLA_Q_LORA) + EPS) * qag_ref[...]
    q = _dot(cqn.astype(bf16), wuq_ref[...])
    q = _head_norm_rope(q, qg_ref[...], ctab, s1, s2)
    q_ref[...] = (q * (MLA_QK ** -0.5 * math.log2(math.e))).astype(bf16)

    ckvn = ckv * lax.rsqrt(jnp.mean(ckv * ckv, axis=-1, keepdims=True) + EPS) * kvag_ref[...]
    ckvb = ckvn.astype(bf16)
    k = _dot(ckvb, wk_ref[...]) + jnp.concatenate([kr] * N_HEADS, axis=1)
    k = _head_norm_rope(k, kg_ref[...], ctab, s1, s2)
    k_ref[...] = k.astype(bf16)
    lane = lax.broadcasted_iota(i32, (1, N_HEADS * HEAD_PAD), 1)
    ones_lane = jnp.where(lane % HEAD_PAD == MLA_V, 1.0, 0.0).astype(f32)
    v_ref[...] = (_dot(ckvb, wv_ref[...]) + ones_lane).astype(bf16)


def _mixer_in(x, sh, sc, g1, win, qag, wuq, kvag, wk, wv, qg, kg, ctab, s1tab, s2tab, seq, tm):
    n = x.shape[0]
    tpb = seq // tm
    hw = N_HEADS * HEAD_PAD
    row = lambda i: (i, 0)
    fixed = lambda i: (0, 0)
    per_b = lambda i: (i // tpb, 0, 0)
    return pl.pallas_call(
        _mixer_in_kernel,
        out_shape=(jax.ShapeDtypeStruct((n, Z_REST), f32),
                   jax.ShapeDtypeStruct((n, hw), bf16),
                   jax.ShapeDtypeStruct((n, hw), bf16),
                   jax.ShapeDtypeStruct((n, hw), bf16)),
        grid=(n // tm,),
        in_specs=[pl.BlockSpec((tm, D_MODEL), row),
                  pl.BlockSpec((1, 1, D_MODEL), per_b),
                  pl.BlockSpec((1, 1, D_MODEL), per_b),
                  pl.BlockSpec((1, D_MODEL), fixed),
                  pl.BlockSpec((D_MODEL, Z_COLS), fixed),
                  pl.BlockSpec((1, 256), fixed),
                  pl.BlockSpec((256, hw), fixed),
                  pl.BlockSpec((1, MLA_KV_LORA), fixed),
                  pl.BlockSpec((MLA_KV_LORA, hw), fixed),
                  pl.BlockSpec((MLA_KV_LORA, hw), fixed),
                  pl.BlockSpec((1, HEAD_PAD), fixed),
                  pl.BlockSpec((1, HEAD_PAD), fixed),
                  pl.BlockSpec((tm, HEAD_PAD), row),
                  pl.BlockSpec((tm, HEAD_PAD), row),
                  pl.BlockSpec((tm, HEAD_PAD), row)],
        out_specs=(pl.BlockSpec((tm, Z_REST), row),
                   pl.BlockSpec((tm, hw), row),
                   pl.BlockSpec((tm, hw), row),
                   pl.BlockSpec((tm, hw), row)),
        compiler_params=_cparams(("arbitrary",)),
        name="mixer_in",
    )(x, sh, sc, g1, win, qag, wuq, kvag, wk, wv, qg, kg, ctab, s1tab, s2tab)


def _attn_kernel(q_ref, k_ref, v_ref, o_ref, m_scr, acc_scr, *, tq):
    qi = pl.program_id(2)
    q = q_ref[0]
    m_scr[...] = jnp.full(m_scr.shape, -jnp.inf, f32)
    acc_scr[...] = jnp.zeros(acc_scr.shape, f32)
    reps = tq // HEAD_PAD

    def step(j, masked):
        start = pl.multiple_of(j * tq, tq)
        kb = k_ref[0, pl.ds(start, tq), :]
        vb = v_ref[0, pl.ds(start, tq), :]
        s = _nt(q, kb)
        if masked:
            r = lax.broadcasted_iota(i32, (tq, tq), 0)
            c = lax.broadcasted_iota(i32, (tq, tq), 1)
            s = jnp.where(c <= r, s, -1e30)
        m_prev = m_scr[...]
        m_new = jnp.maximum(m_prev, jnp.max(s, axis=-1, keepdims=True))
        alpha = jnp.exp2(m_prev - m_new)
        p = jnp.exp2(s - jnp.concatenate([m_new] * reps, axis=1))
        acc_scr[...] = alpha * acc_scr[...] + _dot(p.astype(bf16), vb)
        m_scr[...] = m_new

    def body(j, carry):
        step(j, False)
        return carry

    lax.fori_loop(0, qi, body, 0)
    step(qi, True)
    acc = acc_scr[...]
    o_ref[0] = (acc / acc[:, MLA_V:MLA_V + 1]).astype(bf16)


def _attention(q, k, v, tq):
    b, s, hw = q.shape
    qspec = pl.BlockSpec((1, tq, HEAD_PAD), lambda bi, h, qi: (bi, qi, h))
    kvspec = pl.BlockSpec((1, s, HEAD_PAD), lambda bi, h, qi: (bi, 0, h))
    return pl.pallas_call(
        functools.partial(_attn_kernel, tq=tq),
        out_shape=jax.ShapeDtypeStruct((b, s, hw), bf16),
        grid=(b, N_HEADS, s // tq),
        in_specs=[qspec, kvspec, kvspec],
        out_specs=qspec,
        scratch_shapes=[pltpu.VMEM((tq, HEAD_PAD), f32), pltpu.VMEM((tq, HEAD_PAD), f32)],
        compiler_params=_cparams(("arbitrary", "arbitrary", "arbitrary")),
        name="mla_attention",
    )(q, k, v)


def _seqmix_kernel(z_ref, wpool_ref, pscale_ref, loglb_ref, l1mlb_ref, oml_ref, og_ref, dww_ref, dwb_ref,
                   lng_ref, lnb_ref, le_ref, bones_ref, sel_ref, bdmask_ref, cmask_ref,
                   y_ref, pool_ext, conv_ext, st_scr, *, t):
    step = pl.program_id(1)
    gw = GROUP_WIDTH

    @pl.when(step == 0)
    def _():
        pool_ext[0:16, :] = jnp.zeros((16, gw), f32)
        conv_ext[0:32, :] = jnp.zeros((32, gw), f32)
        st_scr[...] = jnp.zeros(st_scr.shape, f32)

    u = z_ref[:, 0:gw]
    pool_ext[16:16 + t, :] = u
    ext = pool_ext[...]
    sums = {}
    acc = ext
    for w in (1, 2, 4, 8):
        acc = acc + pltpu.roll(acc, w, 0)
        sums[2 * w] = acc[16:16 + t]
    col = lax.broadcasted_iota(i32, (t, gw), 1)
    pos1 = (step * t + lax.broadcasted_iota(i32, (t, gw), 0) + 1).astype(f32)
    wsum = jnp.where(col < 64, sums[2], jnp.where(col < 128, sums[4], jnp.where(col < 192, sums[8], sums[16])))
    wlen = jnp.where(col < 64, 2.0, jnp.where(col < 128, 4.0, jnp.where(col < 192, 8.0, 16.0)))
    pooled = wsum / jnp.minimum(pos1, wlen) - u
    y_ref[:, 0:gw] = (_dot(pooled.astype(bf16), wpool_ref[...]) * pscale_ref[...]).astype(bf16)
    pool_ext[0:16, :] = pool_ext[t:t + 16, :]

    uc = z_ref[:, 5 * gw:6 * gw] * jax.nn.sigmoid(z_ref[:, 6 * gw:7 * gw])
    conv_ext[32:32 + t, :] = uc
    cacc = jnp.zeros((t, gw), f32) + dwb_ref[...]
    for b in range(8):
        n_a = len(range(b, CONV_K, 8))
        xb = conv_ext[pl.ds(32 - (CONV_K - 1) + b, t + 8 * (n_a - 1)), :]
        for a in range(n_a):
            j = 8 * a + b
            cacc = cacc + xb[8 * a:8 * a + t] * dww_ref[j:j + 1, :]
    mu = jnp.mean(cacc, axis=-1, keepdims=True)
    cen = cacc - mu
    var = jnp.mean(cen * cen, axis=-1, keepdims=True)
    cn = cen * lax.rsqrt(var + EPS) * lng_ref[...] + lnb_ref[...]
    y_ref[:, 2 * gw:3 * gw] = _silu(cn).astype(bf16)
    conv_ext[0:32, :] = conv_ext[t:t + 32, :]

    hq = z_ref[:, 1 * gw:2 * gw]
    hf = z_ref[:, 2 * gw:3 * gw]
    v = z_ref[:, 3 * gw:4 * gw]
    hg = z_ref[:, 4 * gw:5 * gw]
    q = _silu(hq)
    ls = jnp.minimum(hf, 0.0) - jnp.log1p(jnp.exp(-jnp.abs(hf)))
    x1 = loglb_ref[...]
    x2 = l1mlb_ref[...] + ls
    logf = jnp.maximum(x1, x2) + jnp.log1p(jnp.exp(-jnp.abs(x1 - x2)))
    kk = oml_ref[...] * jax.nn.sigmoid(-hf)
    le = le_ref[...]
    hi, mid, lo = _split3(logf)
    cs = _dot(le, hi) + _dot(le, mid) + _dot(le, lo)
    bl = cs[0:t]
    be = cs[t:2 * t]
    qt = (q * jnp.exp(bl)).astype(bf16)
    kt = (kk * jnp.exp(be - bl)).astype(bf16)
    dec = jnp.exp(be)
    vb = v.astype(bf16)
    bones = bones_ref[...]
    sel = sel_ref[...]
    bdmask = bdmask_ref[...]
    cmask = cmask_ref[...] > 0.0
    sq = SUB * SUB

    def rep_t(a):
        return jnp.broadcast_to(a[:, None, :], (SUB, SUB, gw)).reshape(sq, gw)

    def rep_s(a):
        return jnp.broadcast_to(a[None, :, :], (SUB, SUB, gw)).reshape(sq, gw)

    st = st_scr[...]
    outs = []
    for j in range(t // SUB):
        rows = slice(j * SUB, (j + 1) * SUB)
        o_inter = _nt(qt[rows], st.astype(bf16))
        diff = rep_t(bl[rows]) - rep_s(bl[rows])
        e = jnp.exp(jnp.where(cmask, diff, -1e30))
        p = (rep_t(q[rows]) * rep_s(kk[rows]) * e).astype(bf16)
        a = _dot(p, bones)
        w = (a * rep_s(v[rows])).astype(bf16)
        o_diag = _dot(sel, w)
        outs.append(o_inter + o_diag)
        st = st * dec[j * SUB:j * SUB + 1, :] + bdmask * _tn(vb[rows], kt[rows])
    st_scr[...] = st
    o = jnp.concatenate(outs, axis=0)
    o2 = o * o
    ohi = o2.astype(bf16)
    olo = (o2 - ohi.astype(f32)).astype(bf16)
    msq = (_dot(ohi, bones) + _dot(olo, bones)) * (1.0 / HGRN_D)
    on = o * lax.rsqrt(msq + EPS) * og_ref[...]
    y_ref[:, gw:2 * gw] = (on * _silu(hg)).astype(bf16)


def _seqmix_consts(t):
    r = jnp.arange(t)
    same = (r[:, None] // SUB) == (r[None, :] // SUB)
    ltri = same & (r[None, :] <= r[:, None])
    le = jnp.concatenate([ltri, same], axis=0).astype(bf16)
    c = jnp.arange(GROUP_WIDTH)
    bd = (c[:, None] // HGRN_D) == (c[None, :] // HGRN_D)
    bones = bd.astype(bf16)
    bdmask = bd.astype(f32)
    p = jnp.arange(SUB * SUB)
    sel = (p[None, :] // SUB == jnp.arange(SUB)[:, None]).astype(bf16)
    cmask = ((p % SUB) <= (p // SUB)).astype(f32)[:, None] * jnp.ones((1, GROUP_WIDTH), f32)
    return le, bones, sel, bdmask, cmask


def _seqmix(zrest, wpool, pscale, loglb, l1mlb, oml, og, dww, dwb, lng, lnb, consts, batch, seq, t):
    n = zrest.shape[0]
    spb = seq // t
    le, bones, sel, bdmask, cmask = consts
    row = lambda b, s: (b * spb + s, 0)
    fixed = lambda b, s: (0, 0)
    vec = pl.BlockSpec((1, GROUP_WIDTH), fixed)
    return pl.pallas_call(
        functools.partial(_seqmix_kernel, t=t),
        out_shape=jax.ShapeDtypeStruct((n, 3 * GROUP_WIDTH), bf16),
        grid=(batch, spb),
        in_specs=[pl.BlockSpec((t, Z_REST), row),
                  pl.BlockSpec((GROUP_WIDTH, GROUP_WIDTH), fixed),
                  vec, vec, vec, vec, vec,
                  pl.BlockSpec((32, GROUP_WIDTH), fixed),
                  vec, vec, vec,
                  pl.BlockSpec((2 * t, t), fixed),
                  pl.BlockSpec((GROUP_WIDTH, GROUP_WIDTH), fixed),
                  pl.BlockSpec((SUB, SUB * SUB), fixed),
                  pl.BlockSpec((GROUP_WIDTH, GROUP_WIDTH), fixed),
                  pl.BlockSpec((SUB * SUB, GROUP_WIDTH), fixed)],
        out_specs=pl.BlockSpec((t, 3 * GROUP_WIDTH), row),
        scratch_shapes=[pltpu.VMEM((t + 16, GROUP_WIDTH), f32),
                        pltpu.VMEM((t + 32, GROUP_WIDTH), f32),
                        pltpu.VMEM((GROUP_WIDTH, GROUP_WIDTH), f32)],
        compiler_params=_cparams(("arbitrary", "arbitrary")),
        name="seq_mixers",
    )(zrest, wpool, pscale, loglb, l1mlb, oml, og, dww, dwb, lng, lnb, le, bones, sel, bdmask, cmask)


def _mixer_out_kernel(o_ref, y_ref, x_ref, gate_ref, wout_ref, n2g_ref, sc_ref, sh_ref, wr_ref, rb_ref,
                      x1_ref, h2_ref, eid_ref, wcol_ref, cnt_ref):
    hw = N_HEADS * HEAD_PAD
    tmix = _dot(o_ref[...], wout_ref[0:hw, :]) + _dot(y_ref[...], wout_ref[hw:, :])
    x1 = x_ref[...] + gate_ref[0] * tmix
    x1_ref[...] = x1
    ms = jnp.mean(x1 * x1, axis=-1, keepdims=True)
    h2 = x1 * lax.rsqrt(ms + EPS) * n2g_ref[...]
    h2 = h2 * (1.0 + sc_ref[0]) + sh_ref[0]
    h2_ref[...] = h2
    hhi = h2.astype(bf16)
    hlo = (h2 - hhi.astype(f32)).astype(bf16)
    wr = wr_ref[...]
    whi = wr.astype(bf16)
    wlo = (wr - whi.astype(f32)).astype(bf16)
    lg = _nt(whi, hhi) + _nt(whi, hlo) + _nt(wlo, hhi) + rb_ref[...]
    tm = lg.shape[1]
    g = [lg[i:i + 1, :] for i in range(N_GROUPS)]
    gmax = jnp.maximum(jnp.maximum(g[0], g[1]), jnp.maximum(g[2], g[3]))
    gidx = jnp.where(g[0] == gmax, 0, jnp.where(g[1] == gmax, 1, jnp.where(g[2] == gmax, 2, 3)))
    gsum = jnp.exp(g[0] - gmax) + jnp.exp(g[1] - gmax) + jnp.exp(g[2] - gmax) + jnp.exp(g[3] - gmax)
    gw_ = 1.0 / gsum
    e = [lg[8 + EXPERTS_PER_GROUP * i:8 + EXPERTS_PER_GROUP * (i + 1), :] for i in range(N_GROUPS)]
    esel = jnp.where(gidx == 0, e[0], jnp.where(gidx == 1, e[1], jnp.where(gidx == 2, e[2], e[3])))
    ri = lax.broadcasted_iota(i32, (EXPERTS_PER_GROUP, tm), 0)
    top1 = jnp.max(esel, axis=0, keepdims=True)
    idx1 = jnp.min(jnp.where(esel == top1, ri, EXPERTS_PER_GROUP), axis=0, keepdims=True)
    rest = jnp.where(ri == idx1, -jnp.inf, esel)
    top2 = jnp.max(rest, axis=0, keepdims=True)
    idx2 = jnp.min(jnp.where(rest == top2, ri, EXPERTS_PER_GROUP), axis=0, keepdims=True)
    e2 = jnp.exp(top2 - top1)
    den = 1.0 + e2
    ex0 = gidx * EXPERTS_PER_GROUP + idx1
    ex1 = gidx * EXPERTS_PER_GROUP + idx2
    eid_ref[...] = jnp.concatenate([ex0, ex1, jnp.zeros((6, tm), i32)], axis=0)
    ew8 = jnp.concatenate([(1.0 / den) * gw_, (e2 / den) * gw_, jnp.zeros((6, tm), f32)], axis=0)
    wcol_ref[...] = ew8.T
    rows = lax.broadcasted_iota(i32, (N_EXPERTS, tm), 0)
    onehot = jnp.where(rows == ex0, 1.0, 0.0) + jnp.where(rows == ex1, 1.0, 0.0)

    @pl.when(pl.program_id(0) == 0)
    def _():
        cnt_ref[...] = jnp.zeros(cnt_ref.shape, f32)

    cnt_ref[...] += _dot(onehot.astype(bf16), jnp.ones((tm, 128), bf16))


def _mixer_out(o, y, x, gate, wout, n2g, sc, sh, wr, rb, seq, tm):
    n = x.shape[0]
    tpb = seq // tm
    hw = N_HEADS * HEAD_PAD
    row = lambda i: (i, 0)
    fixed = lambda i: (0, 0)
    per_b = lambda i: (i // tpb, 0, 0)
    colblk = lambda i: (0, i)
    return pl.pallas_call(
        _mixer_out_kernel,
        out_shape=(jax.ShapeDtypeStruct((n, D_MODEL), f32),
                   jax.ShapeDtypeStruct((n, D_MODEL), f32),
                   jax.ShapeDtypeStruct((8, n), i32),
                   jax.ShapeDtypeStruct((n, 8), f32),
                   jax.ShapeDtypeStruct((N_EXPERTS, 128), f32)),
        grid=(n // tm,),
        in_specs=[pl.BlockSpec((tm, hw), row),
                  pl.BlockSpec((tm, 3 * GROUP_WIDTH), row),
                  pl.BlockSpec((tm, D_MODEL), row),
                  pl.BlockSpec((1, 1, D_MODEL), per_b),
                  pl.BlockSpec((hw + 3 * GROUP_WIDTH, D_MODEL), fixed),
                  pl.BlockSpec((1, D_MODEL), fixed),
                  pl.BlockSpec((1, 1, D_MODEL), per_b),
                  pl.BlockSpec((1, 1, D_MODEL), per_b),
                  pl.BlockSpec((ROUTER_ROWS, D_MODEL), fixed),
                  pl.BlockSpec((ROUTER_ROWS, 1), fixed)],
        out_specs=(pl.BlockSpec((tm, D_MODEL), row),
                   pl.BlockSpec((tm, D_MODEL), row),
                   pl.BlockSpec((8, tm), colblk),
                   pl.BlockSpec((tm, 8), row),
                   pl.BlockSpec((N_EXPERTS, 128), fixed)),
        compiler_params=_cparams(("arbitrary",)),
        name="mixer_out_router",
    )(o, y, x, gate, wout, n2g, sc, sh, wr, rb)


def _moe_rank_kernel(eid_ref, pstart_ref, u_ref, dest_ref, prefix):
    @pl.when(pl.program_id(0) == 0)
    def _():
        prefix[...] = jnp.zeros(prefix.shape, f32)

    tm = eid_ref.shape[1]
    rows = lax.broadcasted_iota(i32, (N_EXPERTS, tm), 0)
    oh0 = jnp.where(rows == eid_ref[0:1, :], 1.0, 0.0)
    oh1 = jnp.where(rows == eid_ref[1:2, :], 1.0, 0.0)
    u = u_ref[...]
    cs0 = _dot(oh0.astype(bf16), u)
    cs1 = _dot(oh1.astype(bf16), u)
    tot0 = cs0[:, tm - 1:tm]
    tot1 = cs1[:, tm - 1:tm]
    base = prefix[...] + pstart_ref[...]
    d0 = jnp.sum(oh0 * (base + cs0 - 1.0), axis=0, keepdims=True)
    d1 = jnp.sum(oh1 * (base + tot0 + cs1 - 1.0), axis=0, keepdims=True)
    prefix[...] = prefix[...] + tot0 + tot1
    dest_ref[...] = jnp.concatenate([d0.astype(i32), d1.astype(i32), jnp.zeros((6, tm), i32)], axis=0)


def _moe_rank(eid, pstart, tm):
    n = eid.shape[1]
    r = jnp.arange(tm)
    u = (r[:, None] <= r[None, :]).astype(bf16)
    return pl.pallas_call(
        _moe_rank_kernel,
        out_shape=jax.ShapeDtypeStruct((8, n), i32),
        grid=(n // tm,),
        in_specs=[pl.BlockSpec((8, tm), lambda i: (0, i)),
                  pl.BlockSpec((N_EXPERTS, 1), lambda i: (0, 0)),
                  pl.BlockSpec((tm, tm), lambda i: (0, 0))],
        out_specs=pl.BlockSpec((8, tm), lambda i: (0, i)),
        scratch_shapes=[pltpu.VMEM((N_EXPERTS, 1), f32)],
        compiler_params=_cparams(("arbitrary",)),
        name="moe_rank",
    )(eid, pstart, u)


def _row_dma(src, src_row, dst, dst_row, sem):
    return pltpu.make_async_copy(src.at[pl.ds(src_row, 1), :], dst.at[pl.ds(dst_row, 1), :], sem)


def _loop(lo, hi, fn, unroll=1):
    def body(r, c):
        fn(r)
        return c
    lax.fori_loop(lo, hi, body, 0, unroll=unroll)


def _dispatch_kernel(plo_ref, phi_ref, d0_ref, d1_ref, h_ref, xs_hbm, zrow, sem, zsem):
    tm = h_ref.shape[0]

    @pl.when(pl.program_id(0) == 0)
    def _():
        zrow[...] = jnp.zeros(zrow.shape, f32)

        def fill(action):
            def per_expert(e):
                _loop(plo_ref[e], phi_ref[e], lambda r: action(_row_dma(zrow, 0, xs_hbm, r, zsem)))
            _loop(0, N_EXPERTS, per_expert)

        fill(lambda cp: cp.start())
        fill(lambda cp: cp.wait())

    def copies(r):
        return (_row_dma(h_ref, r, xs_hbm, d0_ref[0, 0, r], sem), _row_dma(h_ref, r, xs_hbm, d1_ref[0, 0, r], sem))

    def start(r):
        a, b = copies(r)
        a.start()
        b.start()

    def wait(r):
        a, b = copies(r)
        a.wait()
        b.wait()

    _loop(0, tm, start, unroll=8)
    _loop(0, tm, wait, unroll=8)


def _dispatch(h2, d0, d1, pad_lo, pad_hi, n_rows, tm):
    n = h2.shape[0]
    smem_blk = pl.BlockSpec((1, 1, tm), lambda i, lo, hi: (i, 0, 0), memory_space=pltpu.SMEM)
    grid_spec = pltpu.PrefetchScalarGridSpec(
        num_scalar_prefetch=2,
        grid=(n // tm,),
        in_specs=[smem_blk, smem_blk, pl.BlockSpec((tm, D_MODEL), lambda i, lo, hi: (i, 0))],
        out_specs=pl.BlockSpec(memory_space=pl.ANY),
        scratch_shapes=[pltpu.VMEM((8, D_MODEL), f32), pltpu.SemaphoreType.DMA(()), pltpu.SemaphoreType.DMA(())])
    return pl.pallas_call(
        _dispatch_kernel,
        out_shape=jax.ShapeDtypeStruct((n_rows, D_MODEL), f32),
        grid_spec=grid_spec,
        compiler_params=_cparams(("arbitrary",)),
        name="moe_dispatch",
    )(pad_lo, pad_hi, d0, d1, h2)


def _ffn_kernel(be_ref, nu_ref, xs_ref, w1_ref, w3_ref, w2_ref, ys_ref, w1b, w3b, w2b):
    i = pl.program_id(0)
    used = i < nu_ref[0]
    changed = jnp.logical_or(i == 0, be_ref[i] != be_ref[jnp.maximum(i - 1, 0)])

    @pl.when(jnp.logical_and(used, changed))
    def _():
        w1b[...] = w1_ref[0, 0].astype(bf16)
        w3b[...] = w3_ref[0, 0].astype(bf16)
        w2b[...] = w2_ref[0, 0].astype(bf16)

    @pl.when(used)
    def _():
        xb = xs_ref[...].astype(bf16)
        a = _dot(xb, w1b[...])
        b = _dot(xb, w3b[...])
        hb = (_silu(a) * b).astype(bf16)
        ys_ref[...] = _dot(hb, w2b[...])

    @pl.when(jnp.logical_not(used))
    def _():
        ys_ref[...] = jnp.zeros(ys_ref.shape, f32)


def _moe_ffn(xs, blk_expert, n_used, w1, w3, w2, layer):
    n_rows = xs.shape[0]
    nb = n_rows // MOE_BLOCK
    blk = lambda i, be, nu: (jnp.minimum(i, nu[0] - 1), 0)
    wblk = lambda i, be, nu: (layer, be[jnp.minimum(i, nu[0] - 1)], 0, 0)
    grid_spec = pltpu.PrefetchScalarGridSpec(
        num_scalar_prefetch=2,
        grid=(nb,),
        in_specs=[pl.BlockSpec((MOE_BLOCK, D_MODEL), blk),
                  pl.BlockSpec((1, 1, D_MODEL, D_EXPERT), wblk),
                  pl.BlockSpec((1, 1, D_MODEL, D_EXPERT), wblk),
                  pl.BlockSpec((1, 1, D_EXPERT, D_MODEL), wblk)],
        out_specs=pl.BlockSpec((MOE_BLOCK, D_MODEL), lambda i, be, nu: (i, 0)),
        scratch_shapes=[pltpu.VMEM((D_MODEL, D_EXPERT), bf16),
                        pltpu.VMEM((D_MODEL, D_EXPERT), bf16),
                        pltpu.VMEM((D_EXPERT, D_MODEL), bf16)])
    return pl.pallas_call(
        _ffn_kernel,
        out_shape=jax.ShapeDtypeStruct((n_rows, D_MODEL), f32),
        grid_spec=grid_spec,
        compiler_params=_cparams(("arbitrary",)),
        name="moe_ffn",
    )(blk_expert, n_used, xs, w1, w3, w2)


def _combine_kernel(d0c, d1c, d0n, d1n, x_ref, wcol_ref, gate_ref, ys_hbm, o_ref, g0, g1, sem):
    i = pl.program_id(0)
    nb = pl.num_programs(0)
    slot = i % 2
    tm = x_ref.shape[0]

    def copies(d0_ref, d1_ref, s, r):
        return (_row_dma(ys_hbm, d0_ref[0, 0, r], g0.at[s], r, sem.at[s]),
                _row_dma(ys_hbm, d1_ref[0, 0, r], g1.at[s], r, sem.at[s]))

    def start(d0_ref, d1_ref, s):
        def fn(r):
            a, b = copies(d0_ref, d1_ref, s, r)
            a.start()
            b.start()
        _loop(0, tm, fn, unroll=8)

    @pl.when(i == 0)
    def _():
        start(d0c, d1c, 0)

    @pl.when(i + 1 < nb)
    def _():
        start(d0n, d1n, 1 - slot)

    def wait(r):
        a, b = copies(d0c, d1c, slot, r)
        a.wait()
        b.wait()

    _loop(0, tm, wait, unroll=8)
    w = wcol_ref[...]
    y = w[:, 0:1] * g0[slot] + w[:, 1:2] * g1[slot]
    o_ref[...] = x_ref[...] + gate_ref[0] * y


def _combine(x1, wcol, gate, ys, d0, d1, seq, tm):
    n = x1.shape[0]
    nt = n // tm
    tpb = seq // tm
    cur = pl.BlockSpec((1, 1, tm), lambda i: (i, 0, 0), memory_space=pltpu.SMEM)
    nxt = pl.BlockSpec((1, 1, tm), lambda i: (jnp.minimum(i + 1, nt - 1), 0, 0), memory_space=pltpu.SMEM)
    return pl.pallas_call(
        _combine_kernel,
        out_shape=jax.ShapeDtypeStruct((n, D_MODEL), f32),
        grid=(nt,),
        in_specs=[cur, cur, nxt, nxt,
                  pl.BlockSpec((tm, D_MODEL), lambda i: (i, 0)),
                  pl.BlockSpec((tm, 8), lambda i: (i, 0)),
                  pl.BlockSpec((1, 1, D_MODEL), lambda i: (i // tpb, 0, 0)),
                  pl.BlockSpec(memory_space=pl.ANY)],
        out_specs=pl.BlockSpec((tm, D_MODEL), lambda i: (i, 0)),
        scratch_shapes=[pltpu.VMEM((2, tm, D_MODEL), f32), pltpu.VMEM((2, tm, D_MODEL), f32),
                        pltpu.SemaphoreType.DMA((2,))],
        compiler_params=_cparams(("arbitrary",)),
        name="moe_combine",
    )(d0, d1, d0, d1, x1, wcol, gate, ys)


def _moe(h2, x1, eid, wcol, cnt, gate, w1, w3, w2, layer, seq, tm):
    n = h2.shape[0]
    n_rows = 2 * n + N_EXPERTS * MOE_BLOCK
    nb = n_rows // MOE_BLOCK
    counts = cnt[:, 0].astype(i32)
    padded = (counts + MOE_BLOCK - 1) // MOE_BLOCK * MOE_BLOCK
    pend = jnp.cumsum(padded)
    pstart = pend - padded
    n_used = (pend[-1:] // MOE_BLOCK).astype(i32)
    blk_start = jnp.arange(nb, dtype=i32) * MOE_BLOCK
    blk_expert = jnp.minimum(jnp.sum(blk_start[:, None] >= pend[None, :], axis=1), N_EXPERTS - 1).astype(i32)
    dest = _moe_rank(eid, pstart.astype(f32)[:, None], min(512, n))
    d0 = dest[0].reshape(n // tm, 1, tm)
    d1 = dest[1].reshape(n // tm, 1, tm)
    pad_hi = pend.at[N_EXPERTS - 1].set(n_rows).astype(i32)
    xs = _dispatch(h2, d0, d1, (pstart + counts).astype(i32), pad_hi, n_rows, tm)
    ys = _moe_ffn(xs, blk_expert, n_used, w1, w3, w2, layer)
    return _combine(x1, wcol, gate, ys, d0, d1, seq, tm)


def _pad_to(a, axis, size):
    pad = [(0, 0)] * a.ndim
    pad[axis] = (0, size - a.shape[axis])
    return jnp.pad(a, pad)


def _prep_layer_params(w_in, q_a_norm_g, w_uq, kv_a_norm_g, w_ukv, q_norm_g, k_norm_g, w_pool, pool_scale,
                       hgrn_lb_logits, hgrn_out_norm_g, conv_dw_w, w_out, router_group_w, router_group_b,
                       router_expert_w, router_expert_b):
    L = w_in.shape[0]
    zc = lambda n_: jnp.zeros((L, D_MODEL, n_), f32)
    win = jnp.concatenate([w_in[:, :, 352:], w_in[:, :, 0:192], zc(64), w_in[:, :, 192:320], zc(64),
                           w_in[:, :, 320:352], zc(32)], axis=2).astype(bf16)
    qag = _pad_to(q_a_norm_g, 1, 256)[:, None, :]
    wuq = _pad_to(_pad_to(w_uq.reshape(L, MLA_Q_LORA, N_HEADS, MLA_QK), 3, HEAD_PAD), 1, 256)
    wuq = wuq.reshape(L, 256, N_HEADS * HEAD_PAD).astype(bf16)
    kvag = kv_a_norm_g[:, None, :]
    wkv = w_ukv.reshape(L, MLA_KV_LORA, N_HEADS, MLA_NOPE + MLA_V)
    wk = _pad_to(wkv[..., :MLA_NOPE], 3, HEAD_PAD).reshape(L, MLA_KV_LORA, N_HEADS * HEAD_PAD).astype(bf16)
    wv = _pad_to(wkv[..., MLA_NOPE:], 3, HEAD_PAD).reshape(L, MLA_KV_LORA, N_HEADS * HEAD_PAD).astype(bf16)
    qg = _pad_to(q_norm_g, 1, HEAD_PAD)[:, None, :]
    kg = _pad_to(k_norm_g, 1, HEAD_PAD)[:, None, :]
    wpool = jnp.zeros((L, GROUP_WIDTH, GROUP_WIDTH), f32)
    for g in range(len(POOL_WINDOWS)):
        wpool = wpool.at[:, 64 * g:64 * (g + 1), 64 * g:64 * (g + 1)].set(w_pool[:, g])
    wpool = wpool.astype(bf16)
    lb_cum = jnp.cumsum(jax.nn.softmax(hgrn_lb_logits.astype(f32), axis=0), axis=0)
    lb = lb_cum - lb_cum[0:1]
    loglb = jnp.log(lb)[:, None, :]
    l1mlb = jnp.log1p(-lb)[:, None, :]
    oml = (1.0 - lb)[:, None, :]
    og = jnp.tile(hgrn_out_norm_g, (1, N_HEADS))[:, None, :]
    dww = _pad_to(conv_dw_w, 1, 32)
    wo_attn = _pad_to(w_out[:, :GROUP_WIDTH].reshape(L, N_HEADS, MLA_V, D_MODEL), 2, HEAD_PAD)
    wout = jnp.concatenate([wo_attn.reshape(L, N_HEADS * HEAD_PAD, D_MODEL), w_out[:, GROUP_WIDTH:]], axis=1).astype(bf16)
    wr = jnp.concatenate([jnp.swapaxes(router_group_w, 1, 2), jnp.zeros((L, 4, D_MODEL), f32),
                          jnp.swapaxes(router_expert_w, 1, 2)], axis=1)
    rb = jnp.concatenate([router_group_b, jnp.zeros((L, 4), f32), router_expert_b], axis=1)[:, :, None]
    return dict(win=win, qag=qag, wuq=wuq, kvag=kvag, wk=wk, wv=wv, qg=qg, kg=kg, wpool=wpool,
                pscale=pool_scale[:, None, :], loglb=loglb, l1mlb=l1mlb, oml=oml, og=og, dww=dww,
                wout=wout, wr=wr, rb=rb)


def _rope_tabs(positions):
    half = MLA_ROPE // 2
    inv_freq = ROPE_THETA ** (-jnp.arange(half, dtype=f32) / half)
    ang = positions.astype(f32).reshape(-1)[:, None] * inv_freq
    cos = jnp.cos(ang)
    sin = jnp.sin(ang)
    n = cos.shape[0]
    ctab = jnp.concatenate([jnp.ones((n, MLA_NOPE), f32), cos, cos, jnp.zeros((n, 32), f32)], axis=1)
    s1 = jnp.concatenate([jnp.zeros((n, MLA_NOPE + half), f32), sin, jnp.zeros((n, 32), f32)], axis=1)
    s2 = jnp.concatenate([jnp.zeros((n, MLA_NOPE), f32), -sin, jnp.zeros((n, half + 32), f32)], axis=1)
    return ctab, s1, s2


def kernel(x, c, positions, w_ada, b_ada, norm1_g, norm2_g, w_in, q_a_norm_g, w_uq, kv_a_norm_g, w_ukv, q_norm_g, k_norm_g, w_pool, pool_scale, hgrn_lb_logits, hgrn_out_norm_g, conv_dw_w, conv_dw_b, conv_ln_g, conv_ln_b, w_out, router_group_w, router_group_b, router_expert_w, router_expert_b, w1, w3, w2):
    B, S, D = x.shape
    L = w_in.shape[0]
    n = B * S
    tm = min(512, S)
    tq = min(512, S)
    tseq = min(128, S)

    p = _prep_layer_params(w_in, q_a_norm_g, w_uq, kv_a_norm_g, w_ukv, q_norm_g, k_norm_g, w_pool, pool_scale,
                           hgrn_lb_logits, hgrn_out_norm_g, conv_dw_w, w_out, router_group_w, router_group_b,
                           router_expert_w, router_expert_b)
    ctab, s1tab, s2tab = _rope_tabs(positions)
    consts = _seqmix_consts(tseq)
    mod = _ada_mod(c, w_ada, b_ada)
    xf = x.reshape(n, D)
    hw = N_HEADS * HEAD_PAD
    for l in range(L):
        m6 = [mod[l, :, i * D:(i + 1) * D][:, None, :] for i in range(6)]
        sh1, sc1, g1, sh2, sc2, g2 = m6
        zrest, q, k, v = _mixer_in(xf, sh1, sc1, norm1_g[l][None, :], p['win'][l], p['qag'][l], p['wuq'][l],
                                   p['kvag'][l], p['wk'][l], p['wv'][l], p['qg'][l], p['kg'][l],
                                   ctab, s1tab, s2tab, S, tm)
        o = _attention(q.reshape(B, S, hw), k.reshape(B, S, hw), v.reshape(B, S, hw), tq).reshape(n, hw)
        y = _seqmix(zrest, p['wpool'][l], p['pscale'][l], p['loglb'][l], p['l1mlb'][l], p['oml'][l], p['og'][l],
                    p['dww'][l], conv_dw_b[l][None, :], conv_ln_g[l][None, :], conv_ln_b[l][None, :],
                    consts, B, S, tseq)
        x1, h2, eid, wcol, cnt = _mixer_out(o, y, xf, g1, p['wout'][l], norm2_g[l][None, :], sc2, sh2,
                                            p['wr'][l], p['rb'][l], S, tm)
        xf = _moe(h2, x1, eid, wcol, cnt, g2, w1, w3, w2, l, S, tm)
    return xf.reshape(B, S, D)
```

```python
import functools
import math

import jax
import jax.numpy as jnp
from jax import lax
from jax.experimental import pallas as pl
from jax.experimental.pallas import tpu as pltpu

f32 = jnp.float32
bf16 = jnp.bfloat16
i32 = jnp.int32
u32 = jnp.uint32

D_MODEL = 1024
GROUP_WIDTH = 256
N_HEADS = 4
MLA_QK = 96
MLA_NOPE = 64
MLA_ROPE = 32
MLA_V = 64
MLA_Q_LORA = 192
MLA_KV_LORA = 128
HEAD_PAD = 128
HGRN_D = 64
SUB = 16
CONV_K = 31
POOL_WINDOWS = (2, 4, 8, 16)
N_GROUPS = 4
EXPERTS_PER_GROUP = 8
N_EXPERTS = 32
D_EXPERT = 512
MOE_BLOCK = 256
EPS = 1e-6
ROPE_THETA = 10000.0

Z_REST = 7 * GROUP_WIDTH
Z_COLS = Z_REST + 256 + 128 + 128
ROUTER_ROWS = 40

VMEM_LIMIT = 56 * 1024 * 1024


def _cparams(sem):
    return pltpu.CompilerParams(dimension_semantics=sem, vmem_limit_bytes=VMEM_LIMIT)


def _nt(a, b):
    return lax.dot_general(a, b, (((1,), (1,)), ((), ())), preferred_element_type=f32)


def _tn(a, b):
    return lax.dot_general(a, b, (((0,), (0,)), ((), ())), preferred_element_type=f32)


def _dot(a, b):
    return jnp.dot(a, b, preferred_element_type=f32)


def _split3(x):
    hi = x.astype(bf16)
    r1 = x - hi.astype(f32)
    mid = r1.astype(bf16)
    lo = (r1 - mid.astype(f32)).astype(bf16)
    return hi, mid, lo


def _silu(x):
    return x * jax.nn.sigmoid(x)


def _ada_kernel(c_ref, w_ref, b_ref, o_ref):
    c = c_ref[...]
    o_ref[0] = _dot(_silu(c).astype(bf16), w_ref[0].astype(bf16)) + b_ref[0]


def _ada_mod(c, w_ada, b_ada):
    L = w_ada.shape[0]
    B = c.shape[0]
    n6 = w_ada.shape[2]
    tn = 1536
    c8 = jnp.zeros((8, D_MODEL), f32).at[:B].set(c)
    out = pl.pallas_call(
        _ada_kernel,
        out_shape=jax.ShapeDtypeStruct((L, 8, n6), f32),
        grid=(L, n6 // tn),
        in_specs=[pl.BlockSpec((8, D_MODEL), lambda l, j: (0, 0)),
                  pl.BlockSpec((1, D_MODEL, tn), lambda l, j: (l, 0, j)),
                  pl.BlockSpec((1, 1, tn), lambda l, j: (l, 0, j))],
        out_specs=pl.BlockSpec((1, 8, tn), lambda l, j: (l, 0, j)),
        compiler_params=_cparams(("arbitrary", "arbitrary")),
        name="ada_mod",
    )(c8, w_ada, b_ada.reshape(L, 1, n6))
    return out[:, :B]


def _head_norm_rope(t, g, ctab, stab):
    real = lax.broadcasted_iota(i32, (1, HEAD_PAD), 1) < MLA_QK
    segs = []
    for h in range(N_HEADS):
        seg = t[:, h * HEAD_PAD:(h + 1) * HEAD_PAD]
        ms = jnp.sum(jnp.where(real, seg * seg, 0.0), axis=-1, keepdims=True) * (1.0 / MLA_QK)
        tn_ = seg * lax.rsqrt(ms + EPS) * g
        segs.append(tn_ * ctab + pltpu.roll(tn_, HEAD_PAD - MLA_ROPE // 2, 1) * stab)
    return jnp.concatenate(segs, axis=1)


def _mixer_in_kernel(x_ref, sh_ref, sc_ref, g1_ref, win_ref, qag_ref, wuq_ref, kvag_ref, wk_ref, wv_ref,
                     qg_ref, kg_ref, ctab_ref, stab_ref,
                     zrest_ref, q_ref, k_ref, v_ref):
    x = x_ref[...]
    ms = jnp.mean(x * x, axis=-1, keepdims=True)
    h = x * lax.rsqrt(ms + EPS) * g1_ref[...]
    h = h * (1.0 + sc_ref[0]) + sh_ref[0]
    hb = h.astype(bf16)
    zm = _dot(hb, win_ref[:, Z_REST:])
    cq = zm[:, 0:256]
    ckv = zm[:, 256:384]
    kr = zm[:, 384:512]
    ctab = ctab_ref[...]
    stab = stab_ref[...]

    cqn = cq * lax.rsqrt(jnp.sum(cq * cq, axis=-1, keepdims=True) * (1.0 / MLA_Q_LORA) + EPS) * qag_ref[...]
    q = _dot(cqn.astype(bf16), wuq_ref[...])
    q = _head_norm_rope(q, qg_ref[...], ctab, stab)
    q_ref[...] = (q * (MLA_QK ** -0.5 * math.log2(math.e))).astype(bf16)

    ckvn = ckv * lax.rsqrt(jnp.mean(ckv * ckv, axis=-1, keepdims=True) + EPS) * kvag_ref[...]
    ckvb = ckvn.astype(bf16)
    k = _dot(ckvb, wk_ref[...]) + jnp.concatenate([kr] * N_HEADS, axis=1)
    k = _head_norm_rope(k, kg_ref[...], ctab, stab)
    k_ref[...] = k.astype(bf16)
    zrest_ref[...] = _dot(hb, win_ref[:, :Z_REST])
    lane = lax.broadcasted_iota(i32, (1, N_HEADS * HEAD_PAD), 1)
    ones_lane = jnp.where(lane % HEAD_PAD == MLA_V, 1.0, 0.0).astype(f32)
    v_ref[...] = (_dot(ckvb, wv_ref[...]) + ones_lane).astype(bf16)


def _mixer_in(x, sh, sc, g1, win, qag, wuq, kvag, wk, wv, qg, kg, ctab, stab, seq, tm):
    n = x.shape[0]
    tpb = seq // tm
    hw = N_HEADS * HEAD_PAD
    row = lambda i: (i, 0)
    fixed = lambda i: (0, 0)
    per_b = lambda i: (i // tpb, 0, 0)
    return pl.pallas_call(
        _mixer_in_kernel,
        out_shape=(jax.ShapeDtypeStruct((n, Z_REST), f32),
                   jax.ShapeDtypeStruct((n, hw), bf16),
                   jax.ShapeDtypeStruct((n, hw), bf16),
                   jax.ShapeDtypeStruct((n, hw), bf16)),
        grid=(n // tm,),
        in_specs=[pl.BlockSpec((tm, D_MODEL), row),
                  pl.BlockSpec((1, 1, D_MODEL), per_b),
                  pl.BlockSpec((1, 1, D_MODEL), per_b),
                  pl.BlockSpec((1, D_MODEL), fixed),
                  pl.BlockSpec((D_MODEL, Z_COLS), fixed),
                  pl.BlockSpec((1, 256), fixed),
                  pl.BlockSpec((256, hw), fixed),
                  pl.BlockSpec((1, MLA_KV_LORA), fixed),
                  pl.BlockSpec((MLA_KV_LORA, hw), fixed),
                  pl.BlockSpec((MLA_KV_LORA, hw), fixed),
                  pl.BlockSpec((1, HEAD_PAD), fixed),
                  pl.BlockSpec((1, HEAD_PAD), fixed),
                  pl.BlockSpec((tm, HEAD_PAD), row),
                  pl.BlockSpec((tm, HEAD_PAD), row)],
        out_specs=(pl.BlockSpec((tm, Z_REST), row),
                   pl.BlockSpec((tm, hw), row),
                   pl.BlockSpec((tm, hw), row),
                   pl.BlockSpec((tm, hw), row)),
        compiler_params=_cparams(("arbitrary",)),
        name="mixer_in",
    )(x, sh, sc, g1, win, qag, wuq, kvag, wk, wv, qg, kg, ctab, stab)


def _attn_kernel(q_ref, k_ref, v_ref, o_ref, m_scr, acc_scr, *, tq):
    qi = pl.program_id(2)
    q = q_ref[0]
    m_scr[...] = jnp.full(m_scr.shape, -jnp.inf, f32)
    acc_scr[...] = jnp.zeros(acc_scr.shape, f32)
    reps = tq // HEAD_PAD

    def update(m_prev, acc, j, masked):
        start = pl.multiple_of(j * tq, tq)
        kb = k_ref[0, pl.ds(start, tq), :]
        vb = v_ref[0, pl.ds(start, tq), :]
        s = _nt(q, kb)
        if masked:
            r = lax.broadcasted_iota(i32, (tq, tq), 0)
            c = lax.broadcasted_iota(i32, (tq, tq), 1)
            s = jnp.where(c <= r, s, -1e30)
        m_new = jnp.maximum(m_prev, jnp.max(s, axis=-1, keepdims=True))
        alpha = jnp.exp2(m_prev - m_new)
        p = jnp.exp2(s - jnp.concatenate([m_new] * reps, axis=1))
        return m_new, alpha * acc + _dot(p.astype(bf16), vb)

    def steps(js, masked=False):
        m, acc = m_scr[...], acc_scr[...]
        for j in js:
            m, acc = update(m, acc, j, masked)
        m_scr[...] = m
        acc_scr[...] = acc

    def pair(jp, carry):
        steps((2 * jp, 2 * jp + 1))
        return carry

    lax.fori_loop(0, qi // 2, pair, 0)

    @pl.when(qi % 2 == 1)
    def _():
        steps((qi - 1,))

    steps((qi,), masked=True)
    acc = acc_scr[...]
    o_ref[0] = (acc / acc[:, MLA_V:MLA_V + 1]).astype(bf16)


def _attention(q, k, v, tq):
    b, s, hw = q.shape
    qspec = pl.BlockSpec((1, tq, HEAD_PAD), lambda bi, h, qi: (bi, qi, h))
    kvspec = pl.BlockSpec((1, s, HEAD_PAD), lambda bi, h, qi: (bi, 0, h))
    return pl.pallas_call(
        functools.partial(_attn_kernel, tq=tq),
        out_shape=jax.ShapeDtypeStruct((b, s, hw), bf16),
        grid=(b, N_HEADS, s // tq),
        in_specs=[qspec, kvspec, kvspec],
        out_specs=qspec,
        scratch_shapes=[pltpu.VMEM((tq, HEAD_PAD), f32), pltpu.VMEM((tq, HEAD_PAD), f32)],
        compiler_params=_cparams(("arbitrary", "arbitrary", "arbitrary")),
        name="mla_attention",
    )(q, k, v)


def _seqmix_kernel(z_ref, wpool_ref, pscale_ref, loglb_ref, l1mlb_ref, oml_ref, og_ref, dww_ref, dwb_ref,
                   lng_ref, lnb_ref, le_ref, bones_ref, sel_ref, bdmask_ref, cmask_ref,
                   y_ref, pool_ext, conv_ext, st_scr, *, t):
    step = pl.program_id(1)
    gw = GROUP_WIDTH

    @pl.when(step == 0)
    def _():
        pool_ext[0:16, :] = jnp.zeros((16, gw), f32)
        conv_ext[0:32, :] = jnp.zeros((32, gw), f32)
        st_scr[...] = jnp.zeros(st_scr.shape, f32)

    u = z_ref[:, 0:gw]
    pool_ext[16:16 + t, :] = u
    ext = pool_ext[...]
    sums = {}
    acc = ext
    for w in (1, 2, 4, 8):
        acc = acc + pltpu.roll(acc, w, 0)
        sums[2 * w] = acc[16:16 + t]
    col = lax.broadcasted_iota(i32, (t, gw), 1)
    pos1 = (step * t + lax.broadcasted_iota(i32, (t, gw), 0) + 1).astype(f32)
    wsum = jnp.where(col < 64, sums[2], jnp.where(col < 128, sums[4], jnp.where(col < 192, sums[8], sums[16])))
    wlen = jnp.where(col < 64, 2.0, jnp.where(col < 128, 4.0, jnp.where(col < 192, 8.0, 16.0)))
    pooled = wsum / jnp.minimum(pos1, wlen) - u
    y_ref[:, 0:gw] = (_dot(pooled.astype(bf16), wpool_ref[...]) * pscale_ref[...]).astype(bf16)
    pool_ext[0:16, :] = pool_ext[t:t + 16, :]

    uc = z_ref[:, 5 * gw:6 * gw] * jax.nn.sigmoid(z_ref[:, 6 * gw:7 * gw])
    conv_ext[32:32 + t, :] = uc
    cacc = jnp.zeros((t, gw), f32) + dwb_ref[...]
    for b in range(8):
        n_a = len(range(b, CONV_K, 8))
        xb = conv_ext[pl.ds(32 - (CONV_K - 1) + b, t + 8 * (n_a - 1)), :]
        for a in range(n_a):
            j = 8 * a + b
            cacc = cacc + xb[8 * a:8 * a + t] * dww_ref[j:j + 1, :]
    mu = jnp.mean(cacc, axis=-1, keepdims=True)
    cen = cacc - mu
    var = jnp.mean(cen * cen, axis=-1, keepdims=True)
    cn = cen * lax.rsqrt(var + EPS) * lng_ref[...] + lnb_ref[...]
    y_ref[:, 2 * gw:3 * gw] = _silu(cn).astype(bf16)
    conv_ext[0:32, :] = conv_ext[t:t + 32, :]

    hq = z_ref[:, 1 * gw:2 * gw]
    hf = z_ref[:, 2 * gw:3 * gw]
    v = z_ref[:, 3 * gw:4 * gw]
    hg = z_ref[:, 4 * gw:5 * gw]
    q = _silu(hq)
    ls = jnp.minimum(hf, 0.0) - jnp.log1p(jnp.exp(-jnp.abs(hf)))
    x1 = loglb_ref[...]
    x2 = l1mlb_ref[...] + ls
    logf = jnp.maximum(x1, x2) + jnp.log1p(jnp.exp(-jnp.abs(x1 - x2)))
    kk = oml_ref[...] * jax.nn.sigmoid(-hf)
    le = le_ref[...]
    hi, mid, lo = _split3(logf)
    cs = _dot(le, hi) + _dot(le, mid) + _dot(le, lo)
    bl = cs[0:t]
    be = cs[t:2 * t]
    qt = (q * jnp.exp(bl)).astype(bf16)
    kt = (kk * jnp.exp(be - bl)).astype(bf16)
    dec = jnp.exp(be)
    vb = v.astype(bf16)
    bones = bones_ref[...]
    sel = sel_ref[...]
    bdmask = bdmask_ref[...]
    cmask = cmask_ref[...] > 0.0
    sq = SUB * SUB

    def rep_t(a):
        return jnp.broadcast_to(a[:, None, :], (SUB, SUB, gw)).reshape(sq, gw)

    def rep_s(a):
        return jnp.broadcast_to(a[None, :, :], (SUB, SUB, gw)).reshape(sq, gw)

    st = st_scr[...]
    outs = []
    for j in range(t // SUB):
        rows = slice(j * SUB, (j + 1) * SUB)
        o_inter = _nt(qt[rows], st.astype(bf16))
        diff = rep_t(bl[rows]) - rep_s(bl[rows])
        e = jnp.exp(jnp.where(cmask, diff, -1e30))
        p = (rep_t(q[rows]) * rep_s(kk[rows]) * e).astype(bf16)
        a = _dot(p, bones)
        w = (a * rep_s(v[rows])).astype(bf16)
        o_diag = _dot(sel, w)
        outs.append(o_inter + o_diag)
        st = st * dec[j * SUB:j * SUB + 1, :] + bdmask * _tn(vb[rows], kt[rows])
    st_scr[...] = st
    o = jnp.concatenate(outs, axis=0)
    o2 = o * o
    ohi = o2.astype(bf16)
    olo = (o2 - ohi.astype(f32)).astype(bf16)
    msq = (_dot(ohi, bones) + _dot(olo, bones)) * (1.0 / HGRN_D)
    on = o * lax.rsqrt(msq + EPS) * og_ref[...]
    y_ref[:, gw:2 * gw] = (on * _silu(hg)).astype(bf16)


def _seqmix_consts(t):
    r = jnp.arange(t)
    same = (r[:, None] // SUB) == (r[None, :] // SUB)
    ltri = same & (r[None, :] <= r[:, None])
    le = jnp.concatenate([ltri, same], axis=0).astype(bf16)
    c = jnp.arange(GROUP_WIDTH)
    bd = (c[:, None] // HGRN_D) == (c[None, :] // HGRN_D)
    bones = bd.astype(bf16)
    bdmask = bd.astype(f32)
    p = jnp.arange(SUB * SUB)
    sel = (p[None, :] // SUB == jnp.arange(SUB)[:, None]).astype(bf16)
    cmask = ((p % SUB) <= (p // SUB)).astype(f32)[:, None] * jnp.ones((1, GROUP_WIDTH), f32)
    return le, bones, sel, bdmask, cmask


def _seqmix(zrest, wpool, pscale, loglb, l1mlb, oml, og, dww, dwb, lng, lnb, consts, batch, seq, t):
    n = zrest.shape[0]
    spb = seq // t
    le, bones, sel, bdmask, cmask = consts
    row = lambda b, s: (b * spb + s, 0)
    fixed = lambda b, s: (0, 0)
    vec = pl.BlockSpec((1, GROUP_WIDTH), fixed)
    return pl.pallas_call(
        functools.partial(_seqmix_kernel, t=t),
        out_shape=jax.ShapeDtypeStruct((n, 3 * GROUP_WIDTH), bf16),
        grid=(batch, spb),
        in_specs=[pl.BlockSpec((t, Z_REST), row),
                  pl.BlockSpec((GROUP_WIDTH, GROUP_WIDTH), fixed),
                  vec, vec, vec, vec, vec,
                  pl.BlockSpec((32, GROUP_WIDTH), fixed),
                  vec, vec, vec,
                  pl.BlockSpec((2 * t, t), fixed),
                  pl.BlockSpec((GROUP_WIDTH, GROUP_WIDTH), fixed),
                  pl.BlockSpec((SUB, SUB * SUB), fixed),
                  pl.BlockSpec((GROUP_WIDTH, GROUP_WIDTH), fixed),
                  pl.BlockSpec((SUB * SUB, GROUP_WIDTH), fixed)],
        out_specs=pl.BlockSpec((t, 3 * GROUP_WIDTH), row),
        scratch_shapes=[pltpu.VMEM((t + 16, GROUP_WIDTH), f32),
                        pltpu.VMEM((t + 32, GROUP_WIDTH), f32),
                        pltpu.VMEM((GROUP_WIDTH, GROUP_WIDTH), f32)],
        compiler_params=_cparams(("arbitrary", "arbitrary")),
        name="seq_mixers",
    )(zrest, wpool, pscale, loglb, l1mlb, oml, og, dww, dwb, lng, lnb, le, bones, sel, bdmask, cmask)


def _mixer_out_kernel(o_ref, y_ref, x_ref, gate_ref, wout_ref, n2g_ref, sc_ref, sh_ref, wr_ref, rb_ref,
                      x1_ref, h2_ref, eid_ref, wcol_ref, cnt_ref):
    hw = N_HEADS * HEAD_PAD
    tmix = _dot(o_ref[...], wout_ref[0:hw, :]) + _dot(y_ref[...], wout_ref[hw:, :])
    x1 = x_ref[...] + gate_ref[0] * tmix
    x1_ref[...] = x1
    ms = jnp.mean(x1 * x1, axis=-1, keepdims=True)
    h2 = x1 * lax.rsqrt(ms + EPS) * n2g_ref[...]
    h2 = h2 * (1.0 + sc_ref[0]) + sh_ref[0]
    bits = pltpu.bitcast(h2.astype(bf16).astype(f32), u32)
    h2_ref[...] = (bits[:, D_MODEL // 2:] & jnp.uint32(0xFFFF0000)) | (bits[:, :D_MODEL // 2] >> 16)
    hhi = h2.astype(bf16)
    hlo = (h2 - hhi.astype(f32)).astype(bf16)
    wr = wr_ref[...]
    whi = wr.astype(bf16)
    wlo = (wr - whi.astype(f32)).astype(bf16)
    lg = _nt(whi, hhi) + _nt(whi, hlo) + _nt(wlo, hhi) + rb_ref[...]
    tm = lg.shape[1]
    g = [lg[i:i + 1, :] for i in range(N_GROUPS)]
    gmax = jnp.maximum(jnp.maximum(g[0], g[1]), jnp.maximum(g[2], g[3]))
    gidx = jnp.where(g[0] == gmax, 0, jnp.where(g[1] == gmax, 1, jnp.where(g[2] == gmax, 2, 3)))
    gsum = jnp.exp(g[0] - gmax) + jnp.exp(g[1] - gmax) + jnp.exp(g[2] - gmax) + jnp.exp(g[3] - gmax)
    gw_ = 1.0 / gsum
    e = [lg[8 + EXPERTS_PER_GROUP * i:8 + EXPERTS_PER_GROUP * (i + 1), :] for i in range(N_GROUPS)]
    esel = jnp.where(gidx == 0, e[0], jnp.where(gidx == 1, e[1], jnp.where(gidx == 2, e[2], e[3])))
    ri = lax.broadcasted_iota(i32, (EXPERTS_PER_GROUP, tm), 0)
    top1 = jnp.max(esel, axis=0, keepdims=True)
    idx1 = jnp.min(jnp.where(esel == top1, ri, EXPERTS_PER_GROUP), axis=0, keepdims=True)
    rest = jnp.where(ri == idx1, -jnp.inf, esel)
    top2 = jnp.max(rest, axis=0, keepdims=True)
    idx2 = jnp.min(jnp.where(rest == top2, ri, EXPERTS_PER_GROUP), axis=0, keepdims=True)
    e2 = jnp.exp(top2 - top1)
    den = 1.0 + e2
    ex0 = gidx * EXPERTS_PER_GROUP + idx1
    ex1 = gidx * EXPERTS_PER_GROUP + idx2
    eid_ref[...] = jnp.concatenate([ex0, ex1, jnp.zeros((6, tm), i32)], axis=0)
    ew8 = jnp.concatenate([(1.0 / den) * gw_, (e2 / den) * gw_, jnp.zeros((6, tm), f32)], axis=0)
    wcol_ref[...] = ew8.T
    rows = lax.broadcasted_iota(i32, (N_EXPERTS, tm), 0)
    onehot = jnp.where(rows == ex0, 1.0, 0.0) + jnp.where(rows == ex1, 1.0, 0.0)

    @pl.when(pl.program_id(0) == 0)
    def _():
        cnt_ref[...] = jnp.zeros(cnt_ref.shape, f32)

    cnt_ref[...] += _dot(onehot.astype(bf16), jnp.ones((tm, 128), bf16))


def _mixer_out(o, y, x, gate, wout, n2g, sc, sh, wr, rb, seq, tm):
    n = x.shape[0]
    tpb = seq // tm
    hw = N_HEADS * HEAD_PAD
    row = lambda i: (i, 0)
    fixed = lambda i: (0, 0)
    per_b = lambda i: (i // tpb, 0, 0)
    colblk = lambda i: (0, i)
    return pl.pallas_call(
        _mixer_out_kernel,
        out_shape=(jax.ShapeDtypeStruct((n, D_MODEL), f32),
                   jax.ShapeDtypeStruct((n, D_MODEL // 2), u32),
                   jax.ShapeDtypeStruct((8, n), i32),
                   jax.ShapeDtypeStruct((n, 8), f32),
                   jax.ShapeDtypeStruct((N_EXPERTS, 128), f32)),
        grid=(n // tm,),
        in_specs=[pl.BlockSpec((tm, hw), row),
                  pl.BlockSpec((tm, 3 * GROUP_WIDTH), row),
                  pl.BlockSpec((tm, D_MODEL), row),
                  pl.BlockSpec((1, 1, D_MODEL), per_b),
                  pl.BlockSpec((hw + 3 * GROUP_WIDTH, D_MODEL), fixed),
                  pl.BlockSpec((1, D_MODEL), fixed),
                  pl.BlockSpec((1, 1, D_MODEL), per_b),
                  pl.BlockSpec((1, 1, D_MODEL), per_b),
                  pl.BlockSpec((ROUTER_ROWS, D_MODEL), fixed),
                  pl.BlockSpec((ROUTER_ROWS, 1), fixed)],
        out_specs=(pl.BlockSpec((tm, D_MODEL), row),
                   pl.BlockSpec((tm, D_MODEL // 2), row),
                   pl.BlockSpec((8, tm), colblk),
                   pl.BlockSpec((tm, 8), row),
                   pl.BlockSpec((N_EXPERTS, 128), fixed)),
        compiler_params=_cparams(("arbitrary",)),
        name="mixer_out_router",
    )(o, y, x, gate, wout, n2g, sc, sh, wr, rb)


def _moe_rank_kernel(eid_ref, pstart_ref, u_ref, dest_ref, prefix):
    @pl.when(pl.program_id(0) == 0)
    def _():
        prefix[...] = jnp.zeros(prefix.shape, f32)

    tm = eid_ref.shape[1]
    rows = lax.broadcasted_iota(i32, (N_EXPERTS, tm), 0)
    oh0 = jnp.where(rows == eid_ref[0:1, :], 1.0, 0.0)
    oh1 = jnp.where(rows == eid_ref[1:2, :], 1.0, 0.0)
    u = u_ref[...]
    cs0 = _dot(oh0.astype(bf16), u)
    cs1 = _dot(oh1.astype(bf16), u)
    tot0 = cs0[:, tm - 1:tm]
    tot1 = cs1[:, tm - 1:tm]
    base = prefix[...] + pstart_ref[...]
    d0 = jnp.sum(oh0 * (base + cs0 - 1.0), axis=0, keepdims=True)
    d1 = jnp.sum(oh1 * (base + tot0 + cs1 - 1.0), axis=0, keepdims=True)
    prefix[...] = prefix[...] + tot0 + tot1
    dest_ref[...] = jnp.concatenate([d0.astype(i32), d1.astype(i32), jnp.zeros((6, tm), i32)], axis=0)


def _moe_rank(eid, pstart, tm):
    n = eid.shape[1]
    r = jnp.arange(tm)
    u = (r[:, None] <= r[None, :]).astype(bf16)
    return pl.pallas_call(
        _moe_rank_kernel,
        out_shape=jax.ShapeDtypeStruct((8, n), i32),
        grid=(n // tm,),
        in_specs=[pl.BlockSpec((8, tm), lambda i: (0, i)),
                  pl.BlockSpec((N_EXPERTS, 1), lambda i: (0, 0)),
                  pl.BlockSpec((tm, tm), lambda i: (0, 0))],
        out_specs=pl.BlockSpec((8, tm), lambda i: (0, i)),
        scratch_shapes=[pltpu.VMEM((N_EXPERTS, 1), f32)],
        compiler_params=_cparams(("arbitrary",)),
        name="moe_rank",
    )(eid, pstart, u)


def _row_dma(src, src_row, dst, dst_row, sem):
    return pltpu.make_async_copy(src.at[pl.ds(src_row, 1), :], dst.at[pl.ds(dst_row, 1), :], sem)


def _loop(lo, hi, fn, unroll=1):
    def body(r, c):
        fn(r)
        return c
    lax.fori_loop(lo, hi, body, 0, unroll=unroll)


def _dispatch_kernel(plo_ref, phi_ref, d0_ref, d1_ref, h_ref, xs_hbm, zrow, sem, zsem):
    tm = h_ref.shape[0]

    @pl.when(pl.program_id(0) == 0)
    def _():
        zrow[...] = jnp.zeros(zrow.shape, u32)

        def fill(action):
            def per_expert(e):
                _loop(plo_ref[e], phi_ref[e], lambda r: action(_row_dma(zrow, 0, xs_hbm, r, zsem)))
            _loop(0, N_EXPERTS, per_expert)

        fill(lambda cp: cp.start())
        fill(lambda cp: cp.wait())

    def copies(r):
        return (_row_dma(h_ref, r, xs_hbm, d0_ref[0, 0, r], sem), _row_dma(h_ref, r, xs_hbm, d1_ref[0, 0, r], sem))

    def start(r):
        a, b = copies(r)
        a.start(priority=0)
        b.start(priority=1)

    def wait(r):
        a, b = copies(r)
        a.wait()
        b.wait()

    _loop(0, tm, start, unroll=8)
    _loop(0, tm, wait, unroll=8)


def _dispatch(h2, d0, d1, pad_lo, pad_hi, n_rows, tm):
    n = h2.shape[0]
    smem_blk = pl.BlockSpec((1, 1, tm), lambda i, lo, hi: (i, 0, 0), memory_space=pltpu.SMEM)
    grid_spec = pltpu.PrefetchScalarGridSpec(
        num_scalar_prefetch=2,
        grid=(n // tm,),
        in_specs=[smem_blk, smem_blk, pl.BlockSpec((tm, D_MODEL // 2), lambda i, lo, hi: (i, 0))],
        out_specs=pl.BlockSpec(memory_space=pl.ANY),
        scratch_shapes=[pltpu.VMEM((8, D_MODEL // 2), u32), pltpu.SemaphoreType.DMA(()),
                        pltpu.SemaphoreType.DMA(())])
    return pl.pallas_call(
        _dispatch_kernel,
        out_shape=jax.ShapeDtypeStruct((n_rows, D_MODEL // 2), u32),
        grid_spec=grid_spec,
        compiler_params=_cparams(("arbitrary",)),
        name="moe_dispatch",
    )(pad_lo, pad_hi, d0, d1, h2)


def _ffn_kernel(be_ref, nu_ref, xs_ref, w1_ref, w3_ref, w2_ref, ys_ref, w1b, w3b, w2b):
    i = pl.program_id(0)
    used = i < nu_ref[0]
    changed = jnp.logical_or(i == 0, be_ref[i] != be_ref[jnp.maximum(i - 1, 0)])

    @pl.when(jnp.logical_and(used, changed))
    def _():
        w1b[...] = w1_ref[0, 0].astype(bf16)
        w3b[...] = w3_ref[0, 0].astype(bf16)
        w2b[...] = w2_ref[0, 0].astype(bf16)

    @pl.when(used)
    def _():
        packed = xs_ref[...]
        lo = pltpu.bitcast(packed << 16, f32)
        hi = pltpu.bitcast(packed & jnp.uint32(0xFFFF0000), f32)
        xb = jnp.concatenate([lo, hi], axis=1).astype(bf16)
        a = _dot(xb, w1b[...])
        b = _dot(xb, w3b[...])
        hb = (_silu(a) * b).astype(bf16)
        ys_ref[...] = _dot(hb, w2b[...])

    @pl.when(jnp.logical_not(used))
    def _():
        ys_ref[...] = jnp.zeros(ys_ref.shape, f32)


def _moe_ffn(xs, blk_expert, n_used, w1, w3, w2, layer):
    n_rows = xs.shape[0]
    nb = n_rows // MOE_BLOCK
    blk = lambda i, be, nu: (jnp.minimum(i, nu[0] - 1), 0)
    wblk = lambda i, be, nu: (layer, be[jnp.minimum(i, nu[0] - 1)], 0, 0)
    grid_spec = pltpu.PrefetchScalarGridSpec(
        num_scalar_prefetch=2,
        grid=(nb,),
        in_specs=[pl.BlockSpec((MOE_BLOCK, D_MODEL // 2), blk),
                  pl.BlockSpec((1, 1, D_MODEL, D_EXPERT), wblk),
                  pl.BlockSpec((1, 1, D_MODEL, D_EXPERT), wblk),
                  pl.BlockSpec((1, 1, D_EXPERT, D_MODEL), wblk)],
        out_specs=pl.BlockSpec((MOE_BLOCK, D_MODEL), lambda i, be, nu: (i, 0)),
        scratch_shapes=[pltpu.VMEM((D_MODEL, D_EXPERT), bf16),
                        pltpu.VMEM((D_MODEL, D_EXPERT), bf16),
                        pltpu.VMEM((D_EXPERT, D_MODEL), bf16)])
    return pl.pallas_call(
        _ffn_kernel,
        out_shape=jax.ShapeDtypeStruct((n_rows, D_MODEL), f32),
        grid_spec=grid_spec,
        compiler_params=_cparams(("arbitrary",)),
        name="moe_ffn",
    )(blk_expert, n_used, xs, w1, w3, w2)


def _combine_kernel(d0c, d1c, d0n, d1n, x_ref, wcol_ref, gate_ref, ys_hbm, o_ref, g0, g1, sem):
    i = pl.program_id(0)
    nb = pl.num_programs(0)
    slot = i % 2
    tm = x_ref.shape[0]

    def copies(d0_ref, d1_ref, s, r):
        return (_row_dma(ys_hbm, d0_ref[0, 0, r], g0.at[s], r, sem.at[s]),
                _row_dma(ys_hbm, d1_ref[0, 0, r], g1.at[s], r, sem.at[s]))

    def start(d0_ref, d1_ref, s):
        def fn(r):
            a, b = copies(d0_ref, d1_ref, s, r)
            a.start(priority=0)
            b.start(priority=1)
        _loop(0, tm, fn, unroll=8)

    @pl.when(i == 0)
    def _():
        start(d0c, d1c, 0)

    @pl.when(i + 1 < nb)
    def _():
        start(d0n, d1n, 1 - slot)

    def wait(r):
        a, b = copies(d0c, d1c, slot, r)
        a.wait()
        b.wait()

    _loop(0, tm, wait, unroll=8)
    w = wcol_ref[...]
    y = w[:, 0:1] * g0[slot] + w[:, 1:2] * g1[slot]
    o_ref[...] = x_ref[...] + gate_ref[0] * y


def _combine(x1, wcol, gate, ys, d0, d1, seq, tm):
    n = x1.shape[0]
    nt = n // tm
    tpb = seq // tm
    cur = pl.BlockSpec((1, 1, tm), lambda i: (i, 0, 0), memory_space=pltpu.SMEM)
    nxt = pl.BlockSpec((1, 1, tm), lambda i: (jnp.minimum(i + 1, nt - 1), 0, 0), memory_space=pltpu.SMEM)
    return pl.pallas_call(
        _combine_kernel,
        out_shape=jax.ShapeDtypeStruct((n, D_MODEL), f32),
        grid=(nt,),
        in_specs=[cur, cur, nxt, nxt,
                  pl.BlockSpec((tm, D_MODEL), lambda i: (i, 0)),
                  pl.BlockSpec((tm, 8), lambda i: (i, 0)),
                  pl.BlockSpec((1, 1, D_MODEL), lambda i: (i // tpb, 0, 0)),
                  pl.BlockSpec(memory_space=pl.ANY)],
        out_specs=pl.BlockSpec((tm, D_MODEL), lambda i: (i, 0)),
        scratch_shapes=[pltpu.VMEM((2, tm, D_MODEL), f32), pltpu.VMEM((2, tm, D_MODEL), f32),
                        pltpu.SemaphoreType.DMA((2,))],
        compiler_params=_cparams(("arbitrary",)),
        name="moe_combine",
    )(d0, d1, d0, d1, x1, wcol, gate, ys)


def _moe(h2, x1, eid, wcol, cnt, gate, w1, w3, w2, layer, seq, tm):
    n = h2.shape[0]
    n_rows = 2 * n + N_EXPERTS * MOE_BLOCK
    nb = n_rows // MOE_BLOCK
    counts = cnt[:, 0].astype(i32)
    padded = (counts + MOE_BLOCK - 1) // MOE_BLOCK * MOE_BLOCK
    pend = jnp.cumsum(padded)
    pstart = pend - padded
    n_used = (pend[-1:] // MOE_BLOCK).astype(i32)
    blk_start = jnp.arange(nb, dtype=i32) * MOE_BLOCK
    blk_expert = jnp.minimum(jnp.sum(blk_start[:, None] >= pend[None, :], axis=1), N_EXPERTS - 1).astype(i32)
    dest = _moe_rank(eid, pstart.astype(f32)[:, None], min(512, n))
    d0 = dest[0].reshape(n // tm, 1, tm)
    d1 = dest[1].reshape(n // tm, 1, tm)
    pad_hi = pend.at[N_EXPERTS - 1].set(n_rows).astype(i32)
    xs = _dispatch(h2, d0, d1, (pstart + counts).astype(i32), pad_hi, n_rows, tm)
    ys = _moe_ffn(xs, blk_expert, n_used, w1, w3, w2, layer)
    return _combine(x1, wcol, gate, ys, d0, d1, seq, tm)


def _pad_to(a, axis, size):
    pad = [(0, 0)] * a.ndim
    pad[axis] = (0, size - a.shape[axis])
    return jnp.pad(a, pad)


def _prep_layer_params(w_in, q_a_norm_g, w_uq, kv_a_norm_g, w_ukv, q_norm_g, k_norm_g, w_pool, pool_scale,
                       hgrn_lb_logits, hgrn_out_norm_g, conv_dw_w, w_out, router_group_w, router_group_b,
                       router_expert_w, router_expert_b):
    L = w_in.shape[0]
    zc = lambda n_: jnp.zeros((L, D_MODEL, n_), f32)
    win = jnp.concatenate([w_in[:, :, 352:], w_in[:, :, 0:192], zc(64), w_in[:, :, 192:320], zc(64),
                           w_in[:, :, 320:352], w_in[:, :, 320:336], zc(16)], axis=2).astype(bf16)
    qag = _pad_to(q_a_norm_g, 1, 256)[:, None, :]
    half = MLA_ROPE // 2
    x1 = slice(MLA_NOPE, MLA_NOPE + half)
    wuq = w_uq.reshape(L, MLA_Q_LORA, N_HEADS, MLA_QK)
    wuq = _pad_to(_pad_to(jnp.concatenate([wuq, wuq[..., x1]], axis=3), 3, HEAD_PAD), 1, 256)
    wuq = wuq.reshape(L, 256, N_HEADS * HEAD_PAD).astype(bf16)
    kvag = kv_a_norm_g[:, None, :]
    wkv = w_ukv.reshape(L, MLA_KV_LORA, N_HEADS, MLA_NOPE + MLA_V)
    wk = _pad_to(wkv[..., :MLA_NOPE], 3, HEAD_PAD).reshape(L, MLA_KV_LORA, N_HEADS * HEAD_PAD).astype(bf16)
    wv = _pad_to(wkv[..., MLA_NOPE:], 3, HEAD_PAD).reshape(L, MLA_KV_LORA, N_HEADS * HEAD_PAD).astype(bf16)
    qg = _pad_to(jnp.concatenate([q_norm_g, q_norm_g[:, x1]], axis=1), 1, HEAD_PAD)[:, None, :]
    kg = _pad_to(jnp.concatenate([k_norm_g, k_norm_g[:, x1]], axis=1), 1, HEAD_PAD)[:, None, :]
    wpool = jnp.zeros((L, GROUP_WIDTH, GROUP_WIDTH), f32)
    for g in range(len(POOL_WINDOWS)):
        wpool = wpool.at[:, 64 * g:64 * (g + 1), 64 * g:64 * (g + 1)].set(w_pool[:, g])
    wpool = wpool.astype(bf16)
    lb_cum = jnp.cumsum(jax.nn.softmax(hgrn_lb_logits.astype(f32), axis=0), axis=0)
    lb = lb_cum - lb_cum[0:1]
    loglb = jnp.log(lb)[:, None, :]
    l1mlb = jnp.log1p(-lb)[:, None, :]
    oml = (1.0 - lb)[:, None, :]
    og = jnp.tile(hgrn_out_norm_g, (1, N_HEADS))[:, None, :]
    dww = _pad_to(conv_dw_w, 1, 32)
    wo_attn = _pad_to(w_out[:, :GROUP_WIDTH].reshape(L, N_HEADS, MLA_V, D_MODEL), 2, HEAD_PAD)
    wout = jnp.concatenate([wo_attn.reshape(L, N_HEADS * HEAD_PAD, D_MODEL), w_out[:, GROUP_WIDTH:]], axis=1).astype(bf16)
    wr = jnp.concatenate([jnp.swapaxes(router_group_w, 1, 2), jnp.zeros((L, 4, D_MODEL), f32),
                          jnp.swapaxes(router_expert_w, 1, 2)], axis=1)
    rb = jnp.concatenate([router_group_b, jnp.zeros((L, 4), f32), router_expert_b], axis=1)[:, :, None]
    return dict(win=win, qag=qag, wuq=wuq, kvag=kvag, wk=wk, wv=wv, qg=qg, kg=kg, wpool=wpool,
                pscale=pool_scale[:, None, :], loglb=loglb, l1mlb=l1mlb, oml=oml, og=og, dww=dww,
                wout=wout, wr=wr, rb=rb)


def _rope_tabs(positions):
    half = MLA_ROPE // 2
    inv_freq = ROPE_THETA ** (-jnp.arange(half, dtype=f32) / half)
    ang = positions.astype(f32).reshape(-1)[:, None] * inv_freq
    cos = jnp.cos(ang)
    sin = jnp.sin(ang)
    n = cos.shape[0]
    ctab = jnp.concatenate([jnp.ones((n, MLA_NOPE), f32), cos, cos, jnp.zeros((n, 32), f32)], axis=1)
    stab = jnp.concatenate([jnp.zeros((n, MLA_NOPE), f32), -sin, sin, jnp.zeros((n, 32), f32)], axis=1)
    return ctab, stab


def kernel(x, c, positions, w_ada, b_ada, norm1_g, norm2_g, w_in, q_a_norm_g, w_uq, kv_a_norm_g, w_ukv, q_norm_g, k_norm_g, w_pool, pool_scale, hgrn_lb_logits, hgrn_out_norm_g, conv_dw_w, conv_dw_b, conv_ln_g, conv_ln_b, w_out, router_group_w, router_group_b, router_expert_w, router_expert_b, w1, w3, w2):
    B, S, D = x.shape
    L = w_in.shape[0]
    n = B * S
    tm = min(512, S)
    tq = min(512, S)
    tseq = min(128, S)

    p = _prep_layer_params(w_in, q_a_norm_g, w_uq, kv_a_norm_g, w_ukv, q_norm_g, k_norm_g, w_pool, pool_scale,
                           hgrn_lb_logits, hgrn_out_norm_g, conv_dw_w, w_out, router_group_w, router_group_b,
                           router_expert_w, router_expert_b)
    ctab, stab = _rope_tabs(positions)
    consts = _seqmix_consts(tseq)
    mod = _ada_mod(c, w_ada, b_ada)
    xf = x.reshape(n, D)
    hw = N_HEADS * HEAD_PAD
    for l in range(L):
        m6 = [mod[l, :, i * D:(i + 1) * D][:, None, :] for i in range(6)]
        sh1, sc1, g1, sh2, sc2, g2 = m6
        zrest, q, k, v = _mixer_in(xf, sh1, sc1, norm1_g[l][None, :], p['win'][l], p['qag'][l], p['wuq'][l],
                                   p['kvag'][l], p['wk'][l], p['wv'][l], p['qg'][l], p['kg'][l],
                                   ctab, stab, S, tm)
        o = _attention(q.reshape(B, S, hw), k.reshape(B, S, hw), v.reshape(B, S, hw), tq).reshape(n, hw)
        y = _seqmix(zrest, p['wpool'][l], p['pscale'][l], p['loglb'][l], p['l1mlb'][l], p['oml'][l], p['og'][l],
                    p['dww'][l], conv_dw_b[l][None, :], conv_ln_g[l][None, :], conv_ln_b[l][None, :],
                    consts, B, S, tseq)
        x1, h2, eid, wcol, cnt = _mixer_out(o, y, xf, g1, p['wout'][l], norm2_g[l][None, :], sc2, sh2,
                                            p['wr'][l], p['rb'][l], S, tm)
        xf = _moe(h2, x1, eid, wcol, cnt, g2, w1, w3, w2, l, S, tm)
    return xf.reshape(B, S, D)
```

```python
import functools
import math

import jax
import jax.numpy as jnp
from jax import lax
from jax.experimental import pallas as pl
from jax.experimental.pallas import tpu as pltpu

f32 = jnp.float32
bf16 = jnp.bfloat16
i32 = jnp.int32
u32 = jnp.uint32

D_MODEL = 1024
GROUP_WIDTH = 256
N_HEADS = 4
MLA_QK = 96
MLA_NOPE = 64
MLA_ROPE = 32
MLA_V = 64
MLA_Q_LORA = 192
MLA_KV_LORA = 128
HEAD_PAD = 128
HGRN_D = 64
SUB = 16
KEPT = SUB * (SUB // 2) + (SUB // 2) ** 2
CONV_K = 31
POOL_WINDOWS = (2, 4, 8, 16)
N_GROUPS = 4
EXPERTS_PER_GROUP = 8
N_EXPERTS = 32
D_EXPERT = 512
MOE_BLOCK = 256
EPS = 1e-6
ROPE_THETA = 10000.0

Z_REST = 7 * GROUP_WIDTH
Z_COLS = Z_REST + 256 + 128 + 128
ROUTER_ROWS = 40

VMEM_LIMIT = 56 * 1024 * 1024


def _cparams(sem):
    return pltpu.CompilerParams(dimension_semantics=sem, vmem_limit_bytes=VMEM_LIMIT)


def _nt(a, b):
    return lax.dot_general(a, b, (((1,), (1,)), ((), ())), preferred_element_type=f32)


def _tn(a, b):
    return lax.dot_general(a, b, (((0,), (0,)), ((), ())), preferred_element_type=f32)


def _dot(a, b):
    return jnp.dot(a, b, preferred_element_type=f32)


def _split3(x):
    hi = x.astype(bf16)
    r1 = x - hi.astype(f32)
    mid = r1.astype(bf16)
    lo = (r1 - mid.astype(f32)).astype(bf16)
    return hi, mid, lo


def _silu(x):
    return x * jax.nn.sigmoid(x)


def _ada_kernel(c_ref, w_ref, b_ref, o_ref):
    c = c_ref[...]
    o_ref[0] = _dot(_silu(c).astype(bf16), w_ref[0].astype(bf16)) + b_ref[0]


def _ada_mod(c, w_ada, b_ada):
    L = w_ada.shape[0]
    B = c.shape[0]
    n6 = w_ada.shape[2]
    tn = 1536
    c8 = jnp.zeros((8, D_MODEL), f32).at[:B].set(c)
    out = pl.pallas_call(
        _ada_kernel,
        out_shape=jax.ShapeDtypeStruct((L, 8, n6), f32),
        grid=(L, n6 // tn),
        in_specs=[pl.BlockSpec((8, D_MODEL), lambda l, j: (0, 0)),
                  pl.BlockSpec((1, D_MODEL, tn), lambda l, j: (l, 0, j)),
                  pl.BlockSpec((1, 1, tn), lambda l, j: (l, 0, j))],
        out_specs=pl.BlockSpec((1, 8, tn), lambda l, j: (l, 0, j)),
        compiler_params=_cparams(("arbitrary", "arbitrary")),
        name="ada_mod",
    )(c8, w_ada, b_ada.reshape(L, 1, n6))
    return out[:, :B]


def _head_norm_rope(t, g, ctab, stab):
    real = lax.broadcasted_iota(i32, (1, HEAD_PAD), 1) < MLA_QK
    segs = []
    for h in range(N_HEADS):
        seg = t[:, h * HEAD_PAD:(h + 1) * HEAD_PAD]
        ms = jnp.sum(jnp.where(real, seg * seg, 0.0), axis=-1, keepdims=True) * (1.0 / MLA_QK)
        tn_ = seg * lax.rsqrt(ms + EPS) * g
        segs.append(tn_ * ctab + pltpu.roll(tn_, HEAD_PAD - MLA_ROPE // 2, 1) * stab)
    return jnp.concatenate(segs, axis=1)


def _mixer_in_kernel(x_ref, sh_ref, sc_ref, g1_ref, win_ref, qag_ref, wuq_ref, kvag_ref, wk_ref, wv_ref,
                     qg_ref, kg_ref, ctab_ref, stab_ref,
                     zrest_ref, q_ref, k_ref, v_ref):
    x = x_ref[...]
    ms = jnp.mean(x * x, axis=-1, keepdims=True)
    h = x * lax.rsqrt(ms + EPS) * g1_ref[...]
    h = h * (1.0 + sc_ref[0]) + sh_ref[0]
    hb = h.astype(bf16)
    zm = _dot(hb, win_ref[:, Z_REST:])
    cq = zm[:, 0:256]
    ckv = zm[:, 256:384]
    kr = zm[:, 384:512]
    ctab = ctab_ref[...]
    stab = stab_ref[...]

    cqn = cq * lax.rsqrt(jnp.sum(cq * cq, axis=-1, keepdims=True) * (1.0 / MLA_Q_LORA) + EPS) * qag_ref[...]
    q = _dot(cqn.astype(bf16), wuq_ref[...])
    q = _head_norm_rope(q, qg_ref[...], ctab, stab)
    q_ref[...] = (q * (MLA_QK ** -0.5 * math.log2(math.e))).astype(bf16)

    ckvn = ckv * lax.rsqrt(jnp.mean(ckv * ckv, axis=-1, keepdims=True) + EPS) * kvag_ref[...]
    ckvb = ckvn.astype(bf16)
    k = _dot(ckvb, wk_ref[...]) + jnp.concatenate([kr] * N_HEADS, axis=1)
    k = _head_norm_rope(k, kg_ref[...], ctab, stab)
    k_ref[...] = k.astype(bf16)
    zrest_ref[...] = _dot(hb, win_ref[:, :Z_REST])
    lane = lax.broadcasted_iota(i32, (1, N_HEADS * HEAD_PAD), 1)
    ones_lane = jnp.where(lane % HEAD_PAD == MLA_V, 1.0, 0.0).astype(f32)
    v_ref[...] = (_dot(ckvb, wv_ref[...]) + ones_lane).astype(bf16)


def _mixer_in(x, sh, sc, g1, win, qag, wuq, kvag, wk, wv, qg, kg, ctab, stab, seq, tm):
    n = x.shape[0]
    tpb = seq // tm
    hw = N_HEADS * HEAD_PAD
    row = lambda i: (i, 0)
    fixed = lambda i: (0, 0)
    per_b = lambda i: (i // tpb, 0, 0)
    return pl.pallas_call(
        _mixer_in_kernel,
        out_shape=(jax.ShapeDtypeStruct((n, Z_REST), f32),
                   jax.ShapeDtypeStruct((n, hw), bf16),
                   jax.ShapeDtypeStruct((n, hw), bf16),
                   jax.ShapeDtypeStruct((n, hw), bf16)),
        grid=(n // tm,),
        in_specs=[pl.BlockSpec((tm, D_MODEL), row),
                  pl.BlockSpec((1, 1, D_MODEL), per_b),
                  pl.BlockSpec((1, 1, D_MODEL), per_b),
                  pl.BlockSpec((1, D_MODEL), fixed),
                  pl.BlockSpec((D_MODEL, Z_COLS), fixed),
                  pl.BlockSpec((1, 256), fixed),
                  pl.BlockSpec((256, hw), fixed),
                  pl.BlockSpec((1, MLA_KV_LORA), fixed),
                  pl.BlockSpec((MLA_KV_LORA, hw), fixed),
                  pl.BlockSpec((MLA_KV_LORA, hw), fixed),
                  pl.BlockSpec((1, HEAD_PAD), fixed),
                  pl.BlockSpec((1, HEAD_PAD), fixed),
                  pl.BlockSpec((tm, HEAD_PAD), row),
                  pl.BlockSpec((tm, HEAD_PAD), row)],
        out_specs=(pl.BlockSpec((tm, Z_REST), row),
                   pl.BlockSpec((tm, hw), row),
                   pl.BlockSpec((tm, hw), row),
                   pl.BlockSpec((tm, hw), row)),
        compiler_params=_cparams(("arbitrary",)),
        name="mixer_in",
    )(x, sh, sc, g1, win, qag, wuq, kvag, wk, wv, qg, kg, ctab, stab)


def _attn_kernel(q_ref, k_ref, v_ref, o_ref, m_scr, acc_scr, *, tq):
    qi = pl.program_id(2)
    q = q_ref[0]
    m_scr[...] = jnp.full(m_scr.shape, -jnp.inf, f32)
    acc_scr[...] = jnp.zeros(acc_scr.shape, f32)
    reps = tq // HEAD_PAD

    def update(m_prev, acc, j, masked):
        start = pl.multiple_of(j * tq, tq)
        kb = k_ref[0, pl.ds(start, tq), :]
        vb = v_ref[0, pl.ds(start, tq), :]
        s = _nt(q, kb)
        if masked:
            r = lax.broadcasted_iota(i32, (tq, tq), 0)
            c = lax.broadcasted_iota(i32, (tq, tq), 1)
            s = jnp.where(c <= r, s, -1e30)
        m_new = jnp.maximum(m_prev, jnp.max(s, axis=-1, keepdims=True))
        alpha = jnp.exp2(m_prev - m_new)
        p = jnp.exp2(s - jnp.concatenate([m_new] * reps, axis=1))
        return m_new, alpha * acc + _dot(p.astype(bf16), vb)

    def steps(js, masked=False):
        m, acc = m_scr[...], acc_scr[...]
        for j in js:
            m, acc = update(m, acc, j, masked)
        m_scr[...] = m
        acc_scr[...] = acc

    def quad(jq, carry):
        steps(tuple(4 * jq + i for i in range(4)))
        return carry

    lax.fori_loop(0, qi // 4, quad, 0)
    base = (qi // 4) * 4

    @pl.when(qi % 4 >= 2)
    def _():
        steps((base, base + 1))

    @pl.when(qi % 2 == 1)
    def _():
        steps((qi - 1,))

    steps((qi,), masked=True)
    acc = acc_scr[...]
    o_ref[0] = (acc / acc[:, MLA_V:MLA_V + 1]).astype(bf16)


def _attention(q, k, v, tq):
    b, s, hw = q.shape
    qspec = pl.BlockSpec((1, tq, HEAD_PAD), lambda bi, h, qi: (bi, qi, h))
    kvspec = pl.BlockSpec((1, s, HEAD_PAD), lambda bi, h, qi: (bi, 0, h))
    return pl.pallas_call(
        functools.partial(_attn_kernel, tq=tq),
        out_shape=jax.ShapeDtypeStruct((b, s, hw), bf16),
        grid=(b, N_HEADS, s // tq),
        in_specs=[qspec, kvspec, kvspec],
        out_specs=qspec,
        scratch_shapes=[pltpu.VMEM((tq, HEAD_PAD), f32), pltpu.VMEM((tq, HEAD_PAD), f32)],
        compiler_params=_cparams(("arbitrary", "arbitrary", "arbitrary")),
        name="mla_attention",
    )(q, k, v)


def _seqmix_kernel(z_ref, wpool_ref, pscale_ref, loglb_ref, l1mlb_ref, oml_ref, og_ref, dww_ref, dwb_ref,
                   lng_ref, lnb_ref, le_ref, bones_ref, sel_ref, bdmask_ref, cmask_ref,
                   y_ref, pool_ext, conv_ext, st_scr, *, t):
    step = pl.program_id(1)
    gw = GROUP_WIDTH

    @pl.when(step == 0)
    def _():
        pool_ext[0:16, :] = jnp.zeros((16, gw), f32)
        conv_ext[0:32, :] = jnp.zeros((32, gw), f32)
        st_scr[...] = jnp.zeros(st_scr.shape, f32)

    u = z_ref[:, 0:gw]
    pool_ext[16:16 + t, :] = u
    ext = pool_ext[...]
    sums = {}
    acc = ext
    for w in (1, 2, 4, 8):
        acc = acc + pltpu.roll(acc, w, 0)
        sums[2 * w] = acc[16:16 + t]
    col = lax.broadcasted_iota(i32, (t, gw), 1)
    pos1 = (step * t + lax.broadcasted_iota(i32, (t, gw), 0) + 1).astype(f32)
    wsum = jnp.where(col < 64, sums[2], jnp.where(col < 128, sums[4], jnp.where(col < 192, sums[8], sums[16])))
    wlen = jnp.where(col < 64, 2.0, jnp.where(col < 128, 4.0, jnp.where(col < 192, 8.0, 16.0)))
    pooled = wsum / jnp.minimum(pos1, wlen) - u
    y_ref[:, 0:gw] = (_dot(pooled.astype(bf16), wpool_ref[...]) * pscale_ref[...]).astype(bf16)
    pool_ext[0:16, :] = pool_ext[t:t + 16, :]

    uc = z_ref[:, 5 * gw:6 * gw] * jax.nn.sigmoid(z_ref[:, 6 * gw:7 * gw])
    conv_ext[32:32 + t, :] = uc
    cacc = jnp.zeros((t, gw), f32) + dwb_ref[...]
    for b in range(8):
        n_a = len(range(b, CONV_K, 8))
        xb = conv_ext[pl.ds(32 - (CONV_K - 1) + b, t + 8 * (n_a - 1)), :]
        for a in range(n_a):
            j = 8 * a + b
            cacc = cacc + xb[8 * a:8 * a + t] * dww_ref[j:j + 1, :]
    mu = jnp.mean(cacc, axis=-1, keepdims=True)
    cen = cacc - mu
    var = jnp.mean(cen * cen, axis=-1, keepdims=True)
    cn = cen * lax.rsqrt(var + EPS) * lng_ref[...] + lnb_ref[...]
    y_ref[:, 2 * gw:3 * gw] = _silu(cn).astype(bf16)
    conv_ext[0:32, :] = conv_ext[t:t + 32, :]

    hq = z_ref[:, 1 * gw:2 * gw]
    hf = z_ref[:, 2 * gw:3 * gw]
    v = z_ref[:, 3 * gw:4 * gw]
    hg = z_ref[:, 4 * gw:5 * gw]
    q = _silu(hq)
    ls = jnp.minimum(hf, 0.0) - jnp.log1p(jnp.exp(-jnp.abs(hf)))
    x1 = loglb_ref[...]
    x2 = l1mlb_ref[...] + ls
    logf = jnp.maximum(x1, x2) + jnp.log1p(jnp.exp(-jnp.abs(x1 - x2)))
    kk = oml_ref[...] * jax.nn.sigmoid(-hf)
    le = le_ref[...]
    hi, mid, lo = _split3(logf)
    cs = (_dot(le, hi) + _dot(le, mid) + _dot(le, lo)) * math.log2(math.e)
    bl = cs[0:t]
    be = cs[t:2 * t]
    qt = (q * jnp.exp2(bl)).astype(bf16)
    kt = (kk * jnp.exp2(be - bl)).astype(bf16)
    dec = jnp.exp2(be)
    vb = v.astype(bf16)
    bones = bones_ref[...]
    sel = sel_ref[...]
    bdmask = bdmask_ref[...]
    cmask = cmask_ref[...] > 0.0
    half = SUB // 2
    hw2 = gw // 2

    def by_t(x):
        rep = lambda a, nt: jnp.broadcast_to(a[:, None, :], (nt, half, gw)).reshape(nt * half, gw)
        return jnp.concatenate([rep(x, SUB), rep(x[half:SUB], half)], axis=0)

    def by_s(x):
        tile = lambda a, nt: jnp.broadcast_to(a[None, :, :], (nt, half, gw)).reshape(nt * half, gw)
        return jnp.concatenate([tile(x[0:half], SUB), tile(x[half:SUB], half)], axis=0)

    st = [st_scr[0], st_scr[1]]
    outs = []
    for j in range(t // SUB):
        rows = slice(j * SUB, (j + 1) * SUB)
        diff = by_t(bl[rows]) - by_s(bl[rows])
        e = jnp.exp2(jnp.where(cmask, diff, -1e30))
        p = (by_t(q[rows]) * by_s(kk[rows]) * e).astype(bf16)
        a = _dot(p, bones)
        w = (a * by_s(v[rows])).astype(bf16)
        o_j = _dot(sel, w)
        o_inter = []
        for hp in range(2):
            cols = slice(hp * hw2, (hp + 1) * hw2)
            o_inter.append(_nt(qt[rows, cols], st[hp].astype(bf16)))
            st[hp] = st[hp] * dec[j * SUB:j * SUB + 1, cols] + bdmask * _tn(vb[rows, cols], kt[rows, cols])
        outs.append(o_j + jnp.concatenate(o_inter, axis=1))
    st_scr[0] = st[0]
    st_scr[1] = st[1]
    o = jnp.concatenate(outs, axis=0)
    o2 = o * o
    ohi = o2.astype(bf16)
    olo = (o2 - ohi.astype(f32)).astype(bf16)
    msq = (_dot(ohi, bones) + _dot(olo, bones)) * (1.0 / HGRN_D)
    on = o * lax.rsqrt(msq + EPS) * og_ref[...]
    y_ref[:, gw:2 * gw] = (on * _silu(hg)).astype(bf16)


def _seqmix_consts(t):
    r = jnp.arange(t)
    same = (r[:, None] // SUB) == (r[None, :] // SUB)
    ltri = same & (r[None, :] <= r[:, None])
    le = jnp.concatenate([ltri, same], axis=0).astype(bf16)
    c = jnp.arange(GROUP_WIDTH)
    bd = (c[:, None] // HGRN_D) == (c[None, :] // HGRN_D)
    bones = bd.astype(bf16)
    bdmask = bd[:GROUP_WIDTH // 2, :GROUP_WIDTH // 2].astype(f32)
    half = SUB // 2
    r = jnp.arange(KEPT)
    first = r < SUB * half
    t_of = jnp.where(first, r // half, half + (r - SUB * half) // half)
    s_of = jnp.where(first, r % half, half + r % half)
    sel = (t_of[None, :] == jnp.arange(SUB)[:, None]).astype(bf16)
    cmask = (s_of <= t_of).astype(f32)[:, None] * jnp.ones((1, GROUP_WIDTH), f32)
    return le, bones, sel, bdmask, cmask


def _seqmix(zrest, wpool, pscale, loglb, l1mlb, oml, og, dww, dwb, lng, lnb, consts, batch, seq, t):
    n = zrest.shape[0]
    spb = seq // t
    le, bones, sel, bdmask, cmask = consts
    row = lambda b, s: (b * spb + s, 0)
    fixed = lambda b, s: (0, 0)
    vec = pl.BlockSpec((1, GROUP_WIDTH), fixed)
    return pl.pallas_call(
        functools.partial(_seqmix_kernel, t=t),
        out_shape=jax.ShapeDtypeStruct((n, 3 * GROUP_WIDTH), bf16),
        grid=(batch, spb),
        in_specs=[pl.BlockSpec((t, Z_REST), row),
                  pl.BlockSpec((GROUP_WIDTH, GROUP_WIDTH), fixed),
                  vec, vec, vec, vec, vec,
                  pl.BlockSpec((32, GROUP_WIDTH), fixed),
                  vec, vec, vec,
                  pl.BlockSpec((2 * t, t), fixed),
                  pl.BlockSpec((GROUP_WIDTH, GROUP_WIDTH), fixed),
                  pl.BlockSpec((SUB, KEPT), fixed),
                  pl.BlockSpec((GROUP_WIDTH // 2, GROUP_WIDTH // 2), fixed),
                  pl.BlockSpec((KEPT, GROUP_WIDTH), fixed)],
        out_specs=pl.BlockSpec((t, 3 * GROUP_WIDTH), row),
        scratch_shapes=[pltpu.VMEM((t + 16, GROUP_WIDTH), f32),
                        pltpu.VMEM((t + 32, GROUP_WIDTH), f32),
                        pltpu.VMEM((2, GROUP_WIDTH // 2, GROUP_WIDTH // 2), f32)],
        compiler_params=_cparams(("arbitrary", "arbitrary")),
        name="seq_mixers",
    )(zrest, wpool, pscale, loglb, l1mlb, oml, og, dww, dwb, lng, lnb, le, bones, sel, bdmask, cmask)


def _mixer_out_kernel(o_ref, y_ref, x_ref, gate_ref, wout_ref, n2g_ref, sc_ref, sh_ref, wr_ref, rb_ref,
                      x1_ref, h2_ref, eid_ref, wcol_ref, cnt_ref):
    hw = N_HEADS * HEAD_PAD
    wr = wr_ref[...]
    whi = wr.astype(bf16)
    wlo = (wr - whi.astype(f32)).astype(bf16)

    @pl.when(pl.program_id(0) == 0)
    def _():
        cnt_ref[...] = jnp.zeros(cnt_ref.shape, f32)

    tm = x_ref.shape[0]
    for r0 in (0,):
        rs = slice(r0, r0 + tm)
        tmix = _dot(o_ref[rs, :], wout_ref[0:hw, :]) + _dot(y_ref[rs, :], wout_ref[hw:, :])
        x1 = x_ref[rs, :] + gate_ref[0] * tmix
        x1_ref[rs, :] = x1
        ms = jnp.mean(x1 * x1, axis=-1, keepdims=True)
        h2 = x1 * lax.rsqrt(ms + EPS) * n2g_ref[...]
        h2 = h2 * (1.0 + sc_ref[0]) + sh_ref[0]
        hhi = h2.astype(bf16)
        bits = pltpu.bitcast(hhi.astype(f32), u32)
        h2_ref[rs, :] = (bits[:, D_MODEL // 2:] & jnp.uint32(0xFFFF0000)) | (bits[:, :D_MODEL // 2] >> 16)
        hlo = (h2 - hhi.astype(f32)).astype(bf16)
        lg = _nt(whi, hhi) + _nt(whi, hlo) + _nt(wlo, hhi) + rb_ref[...]
        g = [lg[i:i + 1, :] for i in range(N_GROUPS)]
        gmax = jnp.maximum(jnp.maximum(g[0], g[1]), jnp.maximum(g[2], g[3]))
        gidx = jnp.where(g[0] == gmax, 0, jnp.where(g[1] == gmax, 1, jnp.where(g[2] == gmax, 2, 3)))
        gsum = jnp.exp(g[0] - gmax) + jnp.exp(g[1] - gmax) + jnp.exp(g[2] - gmax) + jnp.exp(g[3] - gmax)
        gw_ = 1.0 / gsum
        e = [lg[8 + EXPERTS_PER_GROUP * i:8 + EXPERTS_PER_GROUP * (i + 1), :] for i in range(N_GROUPS)]
        esel = jnp.where(gidx == 0, e[0], jnp.where(gidx == 1, e[1], jnp.where(gidx == 2, e[2], e[3])))
        ri = lax.broadcasted_iota(i32, (EXPERTS_PER_GROUP, tm), 0)
        top1 = jnp.max(esel, axis=0, keepdims=True)
        idx1 = jnp.min(jnp.where(esel == top1, ri, EXPERTS_PER_GROUP), axis=0, keepdims=True)
        rest = jnp.where(ri == idx1, -jnp.inf, esel)
        top2 = jnp.max(rest, axis=0, keepdims=True)
        idx2 = jnp.min(jnp.where(rest == top2, ri, EXPERTS_PER_GROUP), axis=0, keepdims=True)
        e2 = jnp.exp(top2 - top1)
        den = 1.0 + e2
        ex0 = gidx * EXPERTS_PER_GROUP + idx1
        ex1 = gidx * EXPERTS_PER_GROUP + idx2
        eid_ref[:, rs] = jnp.concatenate([ex0, ex1, jnp.zeros((6, tm), i32)], axis=0)
        ew8 = jnp.concatenate([(1.0 / den) * gw_, (e2 / den) * gw_, jnp.zeros((6, tm), f32)], axis=0)
        wcol_ref[rs, :] = ew8.T
        rows = lax.broadcasted_iota(i32, (N_EXPERTS, tm), 0)
        onehot = jnp.where(rows == ex0, 1.0, 0.0) + jnp.where(rows == ex1, 1.0, 0.0)
        cnt_ref[...] += _dot(onehot.astype(bf16), jnp.ones((tm, 128), bf16))


def _mixer_out(o, y, x, gate, wout, n2g, sc, sh, wr, rb, seq, tm):
    n = x.shape[0]
    tpb = seq // tm
    hw = N_HEADS * HEAD_PAD
    row = lambda i: (i, 0)
    fixed = lambda i: (0, 0)
    per_b = lambda i: (i // tpb, 0, 0)
    colblk = lambda i: (0, i)
    return pl.pallas_call(
        _mixer_out_kernel,
        out_shape=(jax.ShapeDtypeStruct((n, D_MODEL), f32),
                   jax.ShapeDtypeStruct((n, D_MODEL // 2), u32),
                   jax.ShapeDtypeStruct((8, n), i32),
                   jax.ShapeDtypeStruct((n, 8), f32),
                   jax.ShapeDtypeStruct((N_EXPERTS, 128), f32)),
        grid=(n // tm,),
        in_specs=[pl.BlockSpec((tm, hw), row),
                  pl.BlockSpec((tm, 3 * GROUP_WIDTH), row),
                  pl.BlockSpec((tm, D_MODEL), row),
                  pl.BlockSpec((1, 1, D_MODEL), per_b),
                  pl.BlockSpec((hw + 3 * GROUP_WIDTH, D_MODEL), fixed),
                  pl.BlockSpec((1, D_MODEL), fixed),
                  pl.BlockSpec((1, 1, D_MODEL), per_b),
                  pl.BlockSpec((1, 1, D_MODEL), per_b),
                  pl.BlockSpec((ROUTER_ROWS, D_MODEL), fixed),
                  pl.BlockSpec((ROUTER_ROWS, 1), fixed)],
        out_specs=(pl.BlockSpec((tm, D_MODEL), row),
                   pl.BlockSpec((tm, D_MODEL // 2), row),
                   pl.BlockSpec((8, tm), colblk),
                   pl.BlockSpec((tm, 8), row),
                   pl.BlockSpec((N_EXPERTS, 128), fixed)),
        compiler_params=_cparams(("arbitrary",)),
        name="mixer_out_router",
    )(o, y, x, gate, wout, n2g, sc, sh, wr, rb)


def _moe_rank_kernel(eid_ref, pstart_ref, u_ref, dest_ref, prefix):
    @pl.when(pl.program_id(0) == 0)
    def _():
        prefix[...] = jnp.zeros(prefix.shape, f32)

    tm = eid_ref.shape[1]
    rows = lax.broadcasted_iota(i32, (N_EXPERTS, tm), 0)
    oh0 = jnp.where(rows == eid_ref[0:1, :], 1.0, 0.0)
    oh1 = jnp.where(rows == eid_ref[1:2, :], 1.0, 0.0)
    u = u_ref[...]
    cs0 = _dot(oh0.astype(bf16), u)
    cs1 = _dot(oh1.astype(bf16), u)
    tot0 = cs0[:, tm - 1:tm]
    tot1 = cs1[:, tm - 1:tm]
    base = prefix[...] + pstart_ref[...]
    d0 = jnp.sum(oh0 * (base + cs0 - 1.0), axis=0, keepdims=True)
    d1 = jnp.sum(oh1 * (base + tot0 + cs1 - 1.0), axis=0, keepdims=True)
    prefix[...] = prefix[...] + tot0 + tot1
    dest_ref[...] = jnp.concatenate([d0.astype(i32), d1.astype(i32), jnp.zeros((6, tm), i32)], axis=0)


def _moe_rank(eid, pstart, tm):
    n = eid.shape[1]
    r = jnp.arange(tm)
    u = (r[:, None] <= r[None, :]).astype(bf16)
    return pl.pallas_call(
        _moe_rank_kernel,
        out_shape=jax.ShapeDtypeStruct((8, n), i32),
        grid=(n // tm,),
        in_specs=[pl.BlockSpec((8, tm), lambda i: (0, i)),
                  pl.BlockSpec((N_EXPERTS, 1), lambda i: (0, 0)),
                  pl.BlockSpec((tm, tm), lambda i: (0, 0))],
        out_specs=pl.BlockSpec((8, tm), lambda i: (0, i)),
        scratch_shapes=[pltpu.VMEM((N_EXPERTS, 1), f32)],
        compiler_params=_cparams(("arbitrary",)),
        name="moe_rank",
    )(eid, pstart, u)


def _row_dma(src, src_row, dst, dst_row, sem):
    return pltpu.make_async_copy(src.at[pl.ds(src_row, 1), :], dst.at[pl.ds(dst_row, 1), :], sem)


def _loop(lo, hi, fn, unroll=1):
    def body(r, c):
        fn(r)
        return c
    lax.fori_loop(lo, hi, body, 0, unroll=unroll)


def _ffn_kernel(be_ref, nu_ref, src_cur, src_nxt, w1_ref, w3_ref, w2_ref, h_hbm, ys_ref, xbuf, gsem, w1b, w3b, w2b):
    i = pl.program_id(0)
    nb = pl.num_programs(0)
    nu = nu_ref[0]
    slot = i % 2
    used = i < nu

    def row_copy(idx_ref, s, r):
        return _row_dma(h_hbm, idx_ref[0, 0, r], xbuf.at[s], r, gsem.at[s])

    def wait_rows(idx_ref, s):
        _loop(0, MOE_BLOCK, lambda r: row_copy(idx_ref, s, r).wait(), unroll=8)

    @pl.when(i == 0)
    def _():
        _loop(0, MOE_BLOCK, lambda r: row_copy(src_cur, 0, r).start(), unroll=8)

    changed = jnp.logical_or(i == 0, be_ref[i] != be_ref[jnp.maximum(i - 1, 0)])

    @pl.when(jnp.logical_and(used, changed))
    def _():
        w1b[...] = w1_ref[0, 0].astype(bf16)
        w3b[...] = w3_ref[0, 0].astype(bf16)
        w2b[...] = w2_ref[0, 0].astype(bf16)

    @pl.when(used)
    def _():
        wait_rows(src_cur, slot)
        packed = xbuf[slot]
        lo = pltpu.bitcast(packed << 16, f32)
        hi = pltpu.bitcast(packed & jnp.uint32(0xFFFF0000), f32)
        xb = jnp.concatenate([lo, hi], axis=1).astype(bf16)
        a = _dot(xb, w1b[...])
        b = _dot(xb, w3b[...])
        hb = (_silu(a) * b).astype(bf16)
        y = _dot(hb, w2b[...])
        for r in range(MOE_BLOCK):
            row_copy(src_nxt, 1 - slot, r).start()
        ys_ref[...] = y

    @pl.when(jnp.logical_and(used, i == nb - 1))
    def _():
        wait_rows(src_nxt, 1 - slot)

    @pl.when(jnp.logical_not(used))
    def _():
        @pl.when(i == nu)
        def _():
            wait_rows(src_cur, slot)

        ys_ref[...] = jnp.zeros(ys_ref.shape, f32)


def _moe_ffn(h2p, row_src, blk_expert, n_used, w1, w3, w2, layer):
    n_rows = row_src.shape[0]
    nb = n_rows // MOE_BLOCK
    src3 = row_src.reshape(nb, 1, MOE_BLOCK)
    smem_blk = lambda f: pl.BlockSpec((1, 1, MOE_BLOCK), f, memory_space=pltpu.SMEM)
    wblk = lambda i, be, nu: (layer, be[jnp.minimum(i, nu[0] - 1)], 0, 0)
    grid_spec = pltpu.PrefetchScalarGridSpec(
        num_scalar_prefetch=2,
        grid=(nb,),
        in_specs=[smem_blk(lambda i, be, nu: (i, 0, 0)),
                  smem_blk(lambda i, be, nu: (jnp.minimum(i + 1, nb - 1), 0, 0)),
                  pl.BlockSpec((1, 1, D_MODEL, D_EXPERT), wblk),
                  pl.BlockSpec((1, 1, D_MODEL, D_EXPERT), wblk),
                  pl.BlockSpec((1, 1, D_EXPERT, D_MODEL), wblk),
                  pl.BlockSpec(memory_space=pl.ANY)],
        out_specs=pl.BlockSpec((MOE_BLOCK, D_MODEL), lambda i, be, nu: (i, 0)),
        scratch_shapes=[pltpu.VMEM((2, MOE_BLOCK, D_MODEL // 2), u32),
                        pltpu.SemaphoreType.DMA((2,)),
                        pltpu.VMEM((D_MODEL, D_EXPERT), bf16),
                        pltpu.VMEM((D_MODEL, D_EXPERT), bf16),
                        pltpu.VMEM((D_EXPERT, D_MODEL), bf16)])
    return pl.pallas_call(
        _ffn_kernel,
        out_shape=jax.ShapeDtypeStruct((n_rows, D_MODEL), f32),
        grid_spec=grid_spec,
        compiler_params=_cparams(("arbitrary",)),
        name="moe_ffn",
    )(blk_expert, n_used, src3, src3, w1, w3, w2, h2p)


def _combine_kernel(d0c, d1c, d0n, d1n, x_ref, wcol_ref, gate_ref, ys_hbm, o_ref, g0, g1, sem):
    i = pl.program_id(0)
    nb = pl.num_programs(0)
    slot = i % 2
    tm = x_ref.shape[0]

    def copies(d0_ref, d1_ref, s, r):
        return (_row_dma(ys_hbm, d0_ref[0, 0, r], g0.at[s], r, sem.at[s]),
                _row_dma(ys_hbm, d1_ref[0, 0, r], g1.at[s], r, sem.at[s]))

    def start(d0_ref, d1_ref, s):
        def fn(r):
            a, b = copies(d0_ref, d1_ref, s, r)
            a.start(priority=0)
            b.start(priority=1)
        _loop(0, tm, fn, unroll=8)

    @pl.when(i == 0)
    def _():
        start(d0c, d1c, 0)

    @pl.when(i + 1 < nb)
    def _():
        start(d0n, d1n, 1 - slot)

    def wait(r):
        a, b = copies(d0c, d1c, slot, r)
        a.wait()
        b.wait()

    _loop(0, tm, wait, unroll=8)
    w = wcol_ref[...]
    y = w[:, 0:1] * g0[slot] + w[:, 1:2] * g1[slot]
    o_ref[...] = x_ref[...] + gate_ref[0] * y


def _combine(x1, wcol, gate, ys, d0, d1, seq, tm):
    n = x1.shape[0]
    nt = n // tm
    tpb = seq // tm
    cur = pl.BlockSpec((1, 1, tm), lambda i: (i, 0, 0), memory_space=pltpu.SMEM)
    nxt = pl.BlockSpec((1, 1, tm), lambda i: (jnp.minimum(i + 1, nt - 1), 0, 0), memory_space=pltpu.SMEM)
    return pl.pallas_call(
        _combine_kernel,
        out_shape=jax.ShapeDtypeStruct((n, D_MODEL), f32),
        grid=(nt,),
        in_specs=[cur, cur, nxt, nxt,
                  pl.BlockSpec((tm, D_MODEL), lambda i: (i, 0)),
                  pl.BlockSpec((tm, 8), lambda i: (i, 0)),
                  pl.BlockSpec((1, 1, D_MODEL), lambda i: (i // tpb, 0, 0)),
                  pl.BlockSpec(memory_space=pl.ANY)],
        out_specs=pl.BlockSpec((tm, D_MODEL), lambda i: (i, 0)),
        scratch_shapes=[pltpu.VMEM((2, tm, D_MODEL), f32), pltpu.VMEM((2, tm, D_MODEL), f32),
                        pltpu.SemaphoreType.DMA((2,))],
        compiler_params=_cparams(("arbitrary",)),
        name="moe_combine",
    )(d0, d1, d0, d1, x1, wcol, gate, ys)


def _moe(h2p, x1, eid, wcol, cnt, gate, w1, w3, w2, layer, seq, tm):
    n = h2p.shape[0]
    n_rows = 2 * n + N_EXPERTS * MOE_BLOCK
    nb = n_rows // MOE_BLOCK
    counts = cnt[:, 0].astype(i32)
    padded = (counts + MOE_BLOCK - 1) // MOE_BLOCK * MOE_BLOCK
    pend = jnp.cumsum(padded)
    pstart = pend - padded
    n_used = (pend[-1:] // MOE_BLOCK).astype(i32)
    blk_start = jnp.arange(nb, dtype=i32) * MOE_BLOCK
    blk_expert = jnp.minimum(jnp.sum(blk_start[:, None] >= pend[None, :], axis=1), N_EXPERTS - 1).astype(i32)
    dest = _moe_rank(eid, pstart.astype(f32)[:, None], min(512, n))
    tok = jnp.arange(n, dtype=i32)
    row_src = jnp.zeros((n_rows,), i32).at[dest[0:2].reshape(-1)].set(jnp.concatenate([tok, tok]), unique_indices=True)
    ys = _moe_ffn(h2p, row_src, blk_expert, n_used, w1, w3, w2, layer)
    d0 = dest[0].reshape(n // tm, 1, tm)
    d1 = dest[1].reshape(n // tm, 1, tm)
    return _combine(x1, wcol, gate, ys, d0, d1, seq, tm)


def _pad_to(a, axis, size):
    pad = [(0, 0)] * a.ndim
    pad[axis] = (0, size - a.shape[axis])
    return jnp.pad(a, pad)


def _prep_layer_params(w_in, q_a_norm_g, w_uq, kv_a_norm_g, w_ukv, q_norm_g, k_norm_g, w_pool, pool_scale,
                       hgrn_lb_logits, hgrn_out_norm_g, conv_dw_w, w_out, router_group_w, router_group_b,
                       router_expert_w, router_expert_b):
    L = w_in.shape[0]
    zc = lambda n_: jnp.zeros((L, D_MODEL, n_), f32)
    win = jnp.concatenate([w_in[:, :, 352:], w_in[:, :, 0:192], zc(64), w_in[:, :, 192:320], zc(64),
                           w_in[:, :, 320:352], w_in[:, :, 320:336], zc(16)], axis=2).astype(bf16)
    qag = _pad_to(q_a_norm_g, 1, 256)[:, None, :]
    half = MLA_ROPE // 2
    x1 = slice(MLA_NOPE, MLA_NOPE + half)
    wuq = w_uq.reshape(L, MLA_Q_LORA, N_HEADS, MLA_QK)
    wuq = _pad_to(_pad_to(jnp.concatenate([wuq, wuq[..., x1]], axis=3), 3, HEAD_PAD), 1, 256)
    wuq = wuq.reshape(L, 256, N_HEADS * HEAD_PAD).astype(bf16)
    kvag = kv_a_norm_g[:, None, :]
    wkv = w_ukv.reshape(L, MLA_KV_LORA, N_HEADS, MLA_NOPE + MLA_V)
    wk = _pad_to(wkv[..., :MLA_NOPE], 3, HEAD_PAD).reshape(L, MLA_KV_LORA, N_HEADS * HEAD_PAD).astype(bf16)
    wv = _pad_to(wkv[..., MLA_NOPE:], 3, HEAD_PAD).reshape(L, MLA_KV_LORA, N_HEADS * HEAD_PAD).astype(bf16)
    qg = _pad_to(jnp.concatenate([q_norm_g, q_norm_g[:, x1]], axis=1), 1, HEAD_PAD)[:, None, :]
    kg = _pad_to(jnp.concatenate([k_norm_g, k_norm_g[:, x1]], axis=1), 1, HEAD_PAD)[:, None, :]
    wpool = jnp.zeros((L, GROUP_WIDTH, GROUP_WIDTH), f32)
    for g in range(len(POOL_WINDOWS)):
        wpool = wpool.at[:, 64 * g:64 * (g + 1), 64 * g:64 * (g + 1)].set(w_pool[:, g])
    wpool = wpool.astype(bf16)
    lb_cum = jnp.cumsum(jax.nn.softmax(hgrn_lb_logits.astype(f32), axis=0), axis=0)
    lb = lb_cum - lb_cum[0:1]
    loglb = jnp.log(lb)[:, None, :]
    l1mlb = jnp.log1p(-lb)[:, None, :]
    oml = (1.0 - lb)[:, None, :]
    og = jnp.tile(hgrn_out_norm_g, (1, N_HEADS))[:, None, :]
    dww = _pad_to(conv_dw_w, 1, 32)
    wo_attn = _pad_to(w_out[:, :GROUP_WIDTH].reshape(L, N_HEADS, MLA_V, D_MODEL), 2, HEAD_PAD)
    wout = jnp.concatenate([wo_attn.reshape(L, N_HEADS * HEAD_PAD, D_MODEL), w_out[:, GROUP_WIDTH:]], axis=1).astype(bf16)
    wr = jnp.concatenate([jnp.swapaxes(router_group_w, 1, 2), jnp.zeros((L, 4, D_MODEL), f32),
                          jnp.swapaxes(router_expert_w, 1, 2)], axis=1)
    rb = jnp.concatenate([router_group_b, jnp.zeros((L, 4), f32), router_expert_b], axis=1)[:, :, None]
    return dict(win=win, qag=qag, wuq=wuq, kvag=kvag, wk=wk, wv=wv, qg=qg, kg=kg, wpool=wpool,
                pscale=pool_scale[:, None, :], loglb=loglb, l1mlb=l1mlb, oml=oml, og=og, dww=dww,
                wout=wout, wr=wr, rb=rb)


def _rope_tabs(positions):
    half = MLA_ROPE // 2
    inv_freq = ROPE_THETA ** (-jnp.arange(half, dtype=f32) / half)
    ang = positions.astype(f32).reshape(-1)[:, None] * inv_freq
    cos = jnp.cos(ang)
    sin = jnp.sin(ang)
    n = cos.shape[0]
    ctab = jnp.concatenate([jnp.ones((n, MLA_NOPE), f32), cos, cos, jnp.zeros((n, 32), f32)], axis=1)
    stab = jnp.concatenate([jnp.zeros((n, MLA_NOPE), f32), -sin, sin, jnp.zeros((n, 32), f32)], axis=1)
    return ctab, stab


def kernel(x, c, positions, w_ada, b_ada, norm1_g, norm2_g, w_in, q_a_norm_g, w_uq, kv_a_norm_g, w_ukv, q_norm_g, k_norm_g, w_pool, pool_scale, hgrn_lb_logits, hgrn_out_norm_g, conv_dw_w, conv_dw_b, conv_ln_g, conv_ln_b, w_out, router_group_w, router_group_b, router_expert_w, router_expert_b, w1, w3, w2):
    B, S, D = x.shape
    L = w_in.shape[0]
    n = B * S
    tm = min(512, S)
    tq = min(512, S)
    tseq = min(128, S)

    p = _prep_layer_params(w_in, q_a_norm_g, w_uq, kv_a_norm_g, w_ukv, q_norm_g, k_norm_g, w_pool, pool_scale,
                           hgrn_lb_logits, hgrn_out_norm_g, conv_dw_w, w_out, router_group_w, router_group_b,
                           router_expert_w, router_expert_b)
    ctab, stab = _rope_tabs(positions)
    consts = _seqmix_consts(tseq)
    mod = _ada_mod(c, w_ada, b_ada)
    xf = x.reshape(n, D)
    hw = N_HEADS * HEAD_PAD
    for l in range(L):
        m6 = [mod[l, :, i * D:(i + 1) * D][:, None, :] for i in range(6)]
        sh1, sc1, g1, sh2, sc2, g2 = m6
        zrest, q, k, v = _mixer_in(xf, sh1, sc1, norm1_g[l][None, :], p['win'][l], p['qag'][l], p['wuq'][l],
                                   p['kvag'][l], p['wk'][l], p['wv'][l], p['qg'][l], p['kg'][l],
                                   ctab, stab, S, tm)
        o = _attention(q.reshape(B, S, hw), k.reshape(B, S, hw), v.reshape(B, S, hw), tq).reshape(n, hw)
        y = _seqmix(zrest, p['wpool'][l], p['pscale'][l], p['loglb'][l], p['l1mlb'][l], p['oml'][l], p['og'][l],
                    p['dww'][l], conv_dw_b[l][None, :], conv_ln_g[l][None, :], conv_ln_b[l][None, :],
                    consts, B, S, tseq)
        x1, h2, eid, wcol, cnt = _mixer_out(o, y, xf, g1, p['wout'][l], norm2_g[l][None, :], sc2, sh2,
                                            p['wr'][l], p['rb'][l], S, tm)
        xf = _moe(h2, x1, eid, wcol, cnt, g2, w1, w3, w2, l, S, tm)
    return xf.reshape(B, S, D)
```

```python
import functools
import math

import jax
import jax.numpy as jnp
from jax import lax
from jax.experimental import pallas as pl
from jax.experimental.pallas import tpu as pltpu

f32 = jnp.float32
bf16 = jnp.bfloat16
i32 = jnp.int32
u32 = jnp.uint32

D_MODEL = 1024
GROUP_WIDTH = 256
N_HEADS = 4
MLA_QK = 96
MLA_NOPE = 64
MLA_ROPE = 32
MLA_V = 64
MLA_Q_LORA = 192
MLA_KV_LORA = 128
HEAD_PAD = 128
HGRN_D = 64
SUB = 16
KEPT = SUB * (SUB // 2) + (SUB // 2) ** 2
CONV_K = 31
POOL_WINDOWS = (2, 4, 8, 16)
N_GROUPS = 4
EXPERTS_PER_GROUP = 8
N_EXPERTS = 32
D_EXPERT = 512
MOE_BLOCK = 256
EPS = 1e-6
ROPE_THETA = 10000.0

Z_REST = 7 * GROUP_WIDTH
Z_COLS = Z_REST + 256 + 128 + 128
ROUTER_ROWS = 40

VMEM_LIMIT = 56 * 1024 * 1024


def _cparams(sem):
    return pltpu.CompilerParams(dimension_semantics=sem, vmem_limit_bytes=VMEM_LIMIT)


def _nt(a, b):
    return lax.dot_general(a, b, (((1,), (1,)), ((), ())), preferred_element_type=f32)


def _tn(a, b):
    return lax.dot_general(a, b, (((0,), (0,)), ((), ())), preferred_element_type=f32)


def _dot(a, b):
    return jnp.dot(a, b, preferred_element_type=f32)


def _split3(x):
    hi = x.astype(bf16)
    r1 = x - hi.astype(f32)
    mid = r1.astype(bf16)
    lo = (r1 - mid.astype(f32)).astype(bf16)
    return hi, mid, lo


def _silu(x):
    return x * jax.nn.sigmoid(x)


def _ada_kernel(c_ref, w_ref, b_ref, o_ref):
    c = c_ref[...]
    o_ref[0] = _dot(_silu(c).astype(bf16), w_ref[0].astype(bf16)) + b_ref[0]


def _ada_mod(c, w_ada, b_ada):
    L = w_ada.shape[0]
    B = c.shape[0]
    n6 = w_ada.shape[2]
    tn = 1536
    c8 = jnp.zeros((8, D_MODEL), f32).at[:B].set(c)
    out = pl.pallas_call(
        _ada_kernel,
        out_shape=jax.ShapeDtypeStruct((L, 8, n6), f32),
        grid=(L, n6 // tn),
        in_specs=[pl.BlockSpec((8, D_MODEL), lambda l, j: (0, 0)),
                  pl.BlockSpec((1, D_MODEL, tn), lambda l, j: (l, 0, j)),
                  pl.BlockSpec((1, 1, tn), lambda l, j: (l, 0, j))],
        out_specs=pl.BlockSpec((1, 8, tn), lambda l, j: (l, 0, j)),
        compiler_params=_cparams(("arbitrary", "arbitrary")),
        name="ada_mod",
    )(c8, w_ada, b_ada.reshape(L, 1, n6))
    return out[:, :B]


def _head_norm_rope(t, g, ctab, stab):
    real = lax.broadcasted_iota(i32, (1, HEAD_PAD), 1) < MLA_QK
    segs = []
    for h in range(N_HEADS):
        seg = t[:, h * HEAD_PAD:(h + 1) * HEAD_PAD]
        ms = jnp.sum(jnp.where(real, seg * seg, 0.0), axis=-1, keepdims=True) * (1.0 / MLA_QK)
        tn_ = seg * lax.rsqrt(ms + EPS) * g
        segs.append(tn_ * ctab + pltpu.roll(tn_, HEAD_PAD - MLA_ROPE // 2, 1) * stab)
    return jnp.concatenate(segs, axis=1)


def _mixer_in_body(x, sh_ref, sc_ref, g1_ref, win_ref, qag_ref, wuq_ref, kvag_ref, wk_ref, wv_ref,
                   qg_ref, kg_ref, ctab_ref, stab_ref, zrest_ref, q_ref, k_ref, v_ref, before_last_store=None):
    ms = jnp.mean(x * x, axis=-1, keepdims=True)
    h = x * lax.rsqrt(ms + EPS) * g1_ref[...]
    h = h * (1.0 + sc_ref[0]) + sh_ref[0]
    hb = h.astype(bf16)
    zm = _dot(hb, win_ref[:, Z_REST:])
    cq = zm[:, 0:256]
    ckv = zm[:, 256:384]
    kr = zm[:, 384:512]
    ctab = ctab_ref[...]
    stab = stab_ref[...]

    cqn = cq * lax.rsqrt(jnp.sum(cq * cq, axis=-1, keepdims=True) * (1.0 / MLA_Q_LORA) + EPS) * qag_ref[...]
    q = _dot(cqn.astype(bf16), wuq_ref[...])
    q = _head_norm_rope(q, qg_ref[...], ctab, stab)
    q_ref[...] = (q * (MLA_QK ** -0.5 * math.log2(math.e))).astype(bf16)

    ckvn = ckv * lax.rsqrt(jnp.mean(ckv * ckv, axis=-1, keepdims=True) + EPS) * kvag_ref[...]
    ckvb = ckvn.astype(bf16)
    k = _dot(ckvb, wk_ref[...]) + jnp.concatenate([kr] * N_HEADS, axis=1)
    k = _head_norm_rope(k, kg_ref[...], ctab, stab)
    k_ref[...] = k.astype(bf16)
    lane = lax.broadcasted_iota(i32, (1, N_HEADS * HEAD_PAD), 1)
    ones_lane = jnp.where(lane % HEAD_PAD == MLA_V, 1.0, 0.0).astype(f32)
    v_ref[...] = (_dot(ckvb, wv_ref[...]) + ones_lane).astype(bf16)
    zrest = _dot(hb, win_ref[:, :Z_REST])
    if before_last_store is not None:
        before_last_store()
    zrest_ref[...] = zrest


def _mixer_in_kernel(x_ref, *refs):
    _mixer_in_body(x_ref[...], *refs)


def _mixer_in_moe_kernel(d0c, d1c, d0n, d1n, x1_ref, wcol_ref, gate_ref, ys_hbm, *refs):
    params, (xout_ref, zrest_ref, q_ref, k_ref, v_ref), (g0, g1, sem) = refs[:13], refs[13:18], refs[18:]
    i = pl.program_id(0)
    nt = pl.num_programs(0)
    slot = i % 2
    tm = x1_ref.shape[0]

    def copies(d0_ref, d1_ref, s, r):
        return (_row_dma(ys_hbm, d0_ref[0, 0, r], g0.at[s], r, sem.at[s]),
                _row_dma(ys_hbm, d1_ref[0, 0, r], g1.at[s], r, sem.at[s]))

    def start(d0_ref, d1_ref, s, r):
        a, b = copies(d0_ref, d1_ref, s, r)
        a.start(priority=0)
        b.start(priority=1)

    def wait(d0_ref, d1_ref, s, r):
        a, b = copies(d0_ref, d1_ref, s, r)
        a.wait()
        b.wait()

    @pl.when(i == 0)
    def _():
        _loop(0, tm, lambda r: start(d0c, d1c, 0, r), unroll=8)

    _loop(0, tm, lambda r: wait(d0c, d1c, slot, r), unroll=8)
    w = wcol_ref[...]
    x = x1_ref[...] + gate_ref[0] * (w[:, 0:1] * g0[slot] + w[:, 1:2] * g1[slot])
    xout_ref[...] = x

    def prefetch():
        for r in range(tm):
            start(d0n, d1n, 1 - slot, r)

    _mixer_in_body(x, *params, zrest_ref, q_ref, k_ref, v_ref, before_last_store=prefetch)

    @pl.when(i == nt - 1)
    def _():
        _loop(0, tm, lambda r: wait(d0n, d1n, 1 - slot, r), unroll=8)


def _mixer_in(x, sh, sc, g1, win, qag, wuq, kvag, wk, wv, qg, kg, ctab, stab, seq, tm, pending=None):
    n = x.shape[0] if pending is None else pending[0].shape[0]
    nt = n // tm
    tpb = seq // tm
    hw = N_HEADS * HEAD_PAD
    row = lambda i: (i, 0)
    fixed = lambda i: (0, 0)
    per_b = lambda i: (i // tpb, 0, 0)
    param_specs = [pl.BlockSpec((1, 1, D_MODEL), per_b),
                   pl.BlockSpec((1, 1, D_MODEL), per_b),
                   pl.BlockSpec((1, D_MODEL), fixed),
                   pl.BlockSpec((D_MODEL, Z_COLS), fixed),
                   pl.BlockSpec((1, 256), fixed),
                   pl.BlockSpec((256, hw), fixed),
                   pl.BlockSpec((1, MLA_KV_LORA), fixed),
                   pl.BlockSpec((MLA_KV_LORA, hw), fixed),
                   pl.BlockSpec((MLA_KV_LORA, hw), fixed),
                   pl.BlockSpec((1, HEAD_PAD), fixed),
                   pl.BlockSpec((1, HEAD_PAD), fixed),
                   pl.BlockSpec((tm, HEAD_PAD), row),
                   pl.BlockSpec((tm, HEAD_PAD), row)]
    params = (sh, sc, g1, win, qag, wuq, kvag, wk, wv, qg, kg, ctab, stab)
    out_shape = (jax.ShapeDtypeStruct((n, Z_REST), f32),
                 jax.ShapeDtypeStruct((n, hw), bf16),
                 jax.ShapeDtypeStruct((n, hw), bf16),
                 jax.ShapeDtypeStruct((n, hw), bf16))
    out_specs = (pl.BlockSpec((tm, Z_REST), row),
                 pl.BlockSpec((tm, hw), row),
                 pl.BlockSpec((tm, hw), row),
                 pl.BlockSpec((tm, hw), row))
    if pending is None:
        return (x,) + tuple(pl.pallas_call(
            _mixer_in_kernel,
            out_shape=out_shape,
            grid=(nt,),
            in_specs=[pl.BlockSpec((tm, D_MODEL), row)] + param_specs,
            out_specs=out_specs,
            compiler_params=_cparams(("arbitrary",)),
            name="mixer_in",
        )(x, *params))
    x1, wcol, gate, ys, d0, d1 = pending
    cur = pl.BlockSpec((1, 1, tm), lambda i: (i, 0, 0), memory_space=pltpu.SMEM)
    nxt = pl.BlockSpec((1, 1, tm), lambda i: (jnp.minimum(i + 1, nt - 1), 0, 0), memory_space=pltpu.SMEM)
    return pl.pallas_call(
        _mixer_in_moe_kernel,
        out_shape=(jax.ShapeDtypeStruct((n, D_MODEL), f32),) + out_shape,
        grid=(nt,),
        in_specs=[cur, cur, nxt, nxt,
                  pl.BlockSpec((tm, D_MODEL), row),
                  pl.BlockSpec((tm, 8), row),
                  pl.BlockSpec((1, 1, D_MODEL), per_b),
                  pl.BlockSpec(memory_space=pl.ANY)] + param_specs,
        out_specs=(pl.BlockSpec((tm, D_MODEL), row),) + out_specs,
        scratch_shapes=[pltpu.VMEM((2, tm, D_MODEL), f32), pltpu.VMEM((2, tm, D_MODEL), f32),
                        pltpu.SemaphoreType.DMA((2,))],
        compiler_params=_cparams(("arbitrary",)),
        name="mixer_in_moe",
    )(d0, d1, d0, d1, x1, wcol, gate, ys, *params)


def _attn_kernel(q_ref, k_ref, v_ref, o_ref, m_scr, acc_scr, *, tq):
    qi = pl.program_id(2)
    q = q_ref[0]
    m_scr[...] = jnp.full(m_scr.shape, -jnp.inf, f32)
    acc_scr[...] = jnp.zeros(acc_scr.shape, f32)
    reps = tq // HEAD_PAD

    def update(m_prev, acc, j, masked):
        start = pl.multiple_of(j * tq, tq)
        kb = k_ref[0, pl.ds(start, tq), :]
        vb = v_ref[0, pl.ds(start, tq), :]
        s = _nt(q, kb)
        if masked:
            r = lax.broadcasted_iota(i32, (tq, tq), 0)
            c = lax.broadcasted_iota(i32, (tq, tq), 1)
            s = jnp.where(c <= r, s, -1e30)
        m_new = jnp.maximum(m_prev, jnp.max(s, axis=-1, keepdims=True))
        alpha = jnp.exp2(m_prev - m_new)
        p = jnp.exp2(s - jnp.concatenate([m_new] * reps, axis=1))
        return m_new, alpha * acc + _dot(p.astype(bf16), vb)

    def steps(js, masked=False):
        m, acc = m_scr[...], acc_scr[...]
        for j in js:
            m, acc = update(m, acc, j, masked)
        m_scr[...] = m
        acc_scr[...] = acc

    def quad(jq, carry):
        steps(tuple(4 * jq + i for i in range(4)))
        return carry

    lax.fori_loop(0, qi // 4, quad, 0)
    base = (qi // 4) * 4

    @pl.when(qi % 4 >= 2)
    def _():
        steps((base, base + 1))

    @pl.when(qi % 2 == 1)
    def _():
        steps((qi - 1,))

    steps((qi,), masked=True)
    acc = acc_scr[...]
    o_ref[0] = (acc / acc[:, MLA_V:MLA_V + 1]).astype(bf16)


def _attention(q, k, v, tq):
    b, s, hw = q.shape
    qspec = pl.BlockSpec((1, tq, HEAD_PAD), lambda bi, h, qi: (bi, qi, h))
    kvspec = pl.BlockSpec((1, s, HEAD_PAD), lambda bi, h, qi: (bi, 0, h))
    return pl.pallas_call(
        functools.partial(_attn_kernel, tq=tq),
        out_shape=jax.ShapeDtypeStruct((b, s, hw), bf16),
        grid=(b, N_HEADS, s // tq),
        in_specs=[qspec, kvspec, kvspec],
        out_specs=qspec,
        scratch_shapes=[pltpu.VMEM((tq, HEAD_PAD), f32), pltpu.VMEM((tq, HEAD_PAD), f32)],
        compiler_params=_cparams(("arbitrary", "arbitrary", "arbitrary")),
        name="mla_attention",
    )(q, k, v)


def _seqmix_kernel(z_ref, wpool_ref, pscale_ref, loglb_ref, l1mlb_ref, oml_ref, og_ref, dww_ref, dwb_ref,
                   lng_ref, lnb_ref, le_ref, bones_ref, sel_ref, bdmask_ref, cmask_ref,
                   y_ref, pool_ext, conv_ext, st_scr, *, t):
    step = pl.program_id(1)
    gw = GROUP_WIDTH

    @pl.when(step == 0)
    def _():
        pool_ext[0:16, :] = jnp.zeros((16, gw), f32)
        conv_ext[0:32, :] = jnp.zeros((32, gw), f32)
        st_scr[...] = jnp.zeros(st_scr.shape, f32)

    u = z_ref[:, 0:gw]
    pool_ext[16:16 + t, :] = u
    ext = pool_ext[...]
    sums = {}
    acc = ext
    for w in (1, 2, 4, 8):
        acc = acc + pltpu.roll(acc, w, 0)
        sums[2 * w] = acc[16:16 + t]
    col = lax.broadcasted_iota(i32, (t, gw), 1)
    pos1 = (step * t + lax.broadcasted_iota(i32, (t, gw), 0) + 1).astype(f32)
    wsum = jnp.where(col < 64, sums[2], jnp.where(col < 128, sums[4], jnp.where(col < 192, sums[8], sums[16])))
    wlen = jnp.where(col < 64, 2.0, jnp.where(col < 128, 4.0, jnp.where(col < 192, 8.0, 16.0)))
    pooled = wsum / jnp.minimum(pos1, wlen) - u
    y_ref[:, 0:gw] = (_dot(pooled.astype(bf16), wpool_ref[...]) * pscale_ref[...]).astype(bf16)
    pool_ext[0:16, :] = pool_ext[t:t + 16, :]

    uc = z_ref[:, 5 * gw:6 * gw] * jax.nn.sigmoid(z_ref[:, 6 * gw:7 * gw])
    conv_ext[32:32 + t, :] = uc
    cacc = jnp.zeros((t, gw), f32) + dwb_ref[...]
    for b in range(8):
        n_a = len(range(b, CONV_K, 8))
        xb = conv_ext[pl.ds(32 - (CONV_K - 1) + b, t + 8 * (n_a - 1)), :]
        for a in range(n_a):
            j = 8 * a + b
            cacc = cacc + xb[8 * a:8 * a + t] * dww_ref[j:j + 1, :]
    mu = jnp.mean(cacc, axis=-1, keepdims=True)
    cen = cacc - mu
    var = jnp.mean(cen * cen, axis=-1, keepdims=True)
    cn = cen * lax.rsqrt(var + EPS) * lng_ref[...] + lnb_ref[...]
    y_ref[:, 2 * gw:3 * gw] = _silu(cn).astype(bf16)
    conv_ext[0:32, :] = conv_ext[t:t + 32, :]

    hq = z_ref[:, 1 * gw:2 * gw]
    hf = z_ref[:, 2 * gw:3 * gw]
    v = z_ref[:, 3 * gw:4 * gw]
    hg = z_ref[:, 4 * gw:5 * gw]
    q = _silu(hq)
    ls = jnp.minimum(hf, 0.0) - jnp.log1p(jnp.exp(-jnp.abs(hf)))
    x1 = loglb_ref[...]
    x2 = l1mlb_ref[...] + ls
    logf = jnp.maximum(x1, x2) + jnp.log1p(jnp.exp(-jnp.abs(x1 - x2)))
    kk = oml_ref[...] * jax.nn.sigmoid(-hf)
    le = le_ref[...]
    hi, mid, lo = _split3(logf)
    cs = (_dot(le, hi) + _dot(le, mid) + _dot(le, lo)) * math.log2(math.e)
    bl = cs[0:t]
    be = cs[t:2 * t]
    qt = (q * jnp.exp2(bl)).astype(bf16)
    kt = (kk * jnp.exp2(be - bl)).astype(bf16)
    dec = jnp.exp2(be)
    vb = v.astype(bf16)
    bones = bones_ref[...]
    sel = sel_ref[...]
    bdmask = bdmask_ref[...]
    cmask = cmask_ref[...] > 0.0
    half = SUB // 2
    hw2 = gw // 2

    def by_t(x):
        rep = lambda a, nt: jnp.broadcast_to(a[:, None, :], (nt, half, gw)).reshape(nt * half, gw)
        return jnp.concatenate([rep(x, SUB), rep(x[half:SUB], half)], axis=0)

    def by_s(x):
        tile = lambda a, nt: jnp.broadcast_to(a[None, :, :], (nt, half, gw)).reshape(nt * half, gw)
        return jnp.concatenate([tile(x[0:half], SUB), tile(x[half:SUB], half)], axis=0)

    st = [st_scr[0], st_scr[1]]
    outs = []
    for j in range(t // SUB):
        rows = slice(j * SUB, (j + 1) * SUB)
        diff = by_t(bl[rows]) - by_s(bl[rows])
        e = jnp.exp2(jnp.where(cmask, diff, -1e30))
        p = (by_t(q[rows]) * by_s(kk[rows]) * e).astype(bf16)
        a = _dot(p, bones)
        w = (a * by_s(v[rows])).astype(bf16)
        o_j = _dot(sel, w)
        o_inter = []
        for hp in range(2):
            cols = slice(hp * hw2, (hp + 1) * hw2)
            o_inter.append(_nt(qt[rows, cols], st[hp].astype(bf16)))
            st[hp] = st[hp] * dec[j * SUB:j * SUB + 1, cols] + bdmask * _tn(vb[rows, cols], kt[rows, cols])
        outs.append(o_j + jnp.concatenate(o_inter, axis=1))
    st_scr[0] = st[0]
    st_scr[1] = st[1]
    o = jnp.concatenate(outs, axis=0)
    o2 = o * o
    ohi = o2.astype(bf16)
    olo = (o2 - ohi.astype(f32)).astype(bf16)
    msq = (_dot(ohi, bones) + _dot(olo, bones)) * (1.0 / HGRN_D)
    on = o * lax.rsqrt(msq + EPS) * og_ref[...]
    y_ref[:, gw:2 * gw] = (on * _silu(hg)).astype(bf16)


def _seqmix_consts(t):
    r = jnp.arange(t)
    same = (r[:, None] // SUB) == (r[None, :] // SUB)
    ltri = same & (r[None, :] <= r[:, None])
    le = jnp.concatenate([ltri, same], axis=0).astype(bf16)
    c = jnp.arange(GROUP_WIDTH)
    bd = (c[:, None] // HGRN_D) == (c[None, :] // HGRN_D)
    bones = bd.astype(bf16)
    bdmask = bd[:GROUP_WIDTH // 2, :GROUP_WIDTH // 2].astype(f32)
    half = SUB // 2
    r = jnp.arange(KEPT)
    first = r < SUB * half
    t_of = jnp.where(first, r // half, half + (r - SUB * half) // half)
    s_of = jnp.where(first, r % half, half + r % half)
    sel = (t_of[None, :] == jnp.arange(SUB)[:, None]).astype(bf16)
    cmask = (s_of <= t_of).astype(f32)[:, None] * jnp.ones((1, GROUP_WIDTH), f32)
    return le, bones, sel, bdmask, cmask


def _seqmix(zrest, wpool, pscale, loglb, l1mlb, oml, og, dww, dwb, lng, lnb, consts, batch, seq, t):
    n = zrest.shape[0]
    spb = seq // t
    le, bones, sel, bdmask, cmask = consts
    row = lambda b, s: (b * spb + s, 0)
    fixed = lambda b, s: (0, 0)
    vec = pl.BlockSpec((1, GROUP_WIDTH), fixed)
    return pl.pallas_call(
        functools.partial(_seqmix_kernel, t=t),
        out_shape=jax.ShapeDtypeStruct((n, 3 * GROUP_WIDTH), bf16),
        grid=(batch, spb),
        in_specs=[pl.BlockSpec((t, Z_REST), row),
                  pl.BlockSpec((GROUP_WIDTH, GROUP_WIDTH), fixed),
                  vec, vec, vec, vec, vec,
                  pl.BlockSpec((32, GROUP_WIDTH), fixed),
                  vec, vec, vec,
                  pl.BlockSpec((2 * t, t), fixed),
                  pl.BlockSpec((GROUP_WIDTH, GROUP_WIDTH), fixed),
                  pl.BlockSpec((SUB, KEPT), fixed),
                  pl.BlockSpec((GROUP_WIDTH // 2, GROUP_WIDTH // 2), fixed),
                  pl.BlockSpec((KEPT, GROUP_WIDTH), fixed)],
        out_specs=pl.BlockSpec((t, 3 * GROUP_WIDTH), row),
        scratch_shapes=[pltpu.VMEM((t + 16, GROUP_WIDTH), f32),
                        pltpu.VMEM((t + 32, GROUP_WIDTH), f32),
                        pltpu.VMEM((2, GROUP_WIDTH // 2, GROUP_WIDTH // 2), f32)],
        compiler_params=_cparams(("arbitrary", "arbitrary")),
        name="seq_mixers",
    )(zrest, wpool, pscale, loglb, l1mlb, oml, og, dww, dwb, lng, lnb, le, bones, sel, bdmask, cmask)


def _mixer_out_kernel(o_ref, y_ref, x_ref, gate_ref, wout_ref, n2g_ref, sc_ref, sh_ref, wr_ref, rb_ref,
                      x1_ref, h2_ref, eid_ref, wcol_ref, cnt_ref):
    hw = N_HEADS * HEAD_PAD
    wr = wr_ref[...]
    whi = wr.astype(bf16)
    wlo = (wr - whi.astype(f32)).astype(bf16)

    @pl.when(pl.program_id(0) == 0)
    def _():
        cnt_ref[...] = jnp.zeros(cnt_ref.shape, f32)

    tm = x_ref.shape[0]
    for r0 in (0,):
        rs = slice(r0, r0 + tm)
        tmix = _dot(o_ref[rs, :], wout_ref[0:hw, :]) + _dot(y_ref[rs, :], wout_ref[hw:, :])
        x1 = x_ref[rs, :] + gate_ref[0] * tmix
        x1_ref[rs, :] = x1
        ms = jnp.mean(x1 * x1, axis=-1, keepdims=True)
        h2 = x1 * lax.rsqrt(ms + EPS) * n2g_ref[...]
        h2 = h2 * (1.0 + sc_ref[0]) + sh_ref[0]
        hhi = h2.astype(bf16)
        bits = pltpu.bitcast(hhi.astype(f32), u32)
        h2_ref[rs, :] = (bits[:, D_MODEL // 2:] & jnp.uint32(0xFFFF0000)) | (bits[:, :D_MODEL // 2] >> 16)
        hlo = (h2 - hhi.astype(f32)).astype(bf16)
        lg = _nt(whi, hhi) + _nt(whi, hlo) + _nt(wlo, hhi) + rb_ref[...]
        g = [lg[i:i + 1, :] for i in range(N_GROUPS)]
        gmax = jnp.maximum(jnp.maximum(g[0], g[1]), jnp.maximum(g[2], g[3]))
        gidx = jnp.where(g[0] == gmax, 0, jnp.where(g[1] == gmax, 1, jnp.where(g[2] == gmax, 2, 3)))
        gsum = jnp.exp(g[0] - gmax) + jnp.exp(g[1] - gmax) + jnp.exp(g[2] - gmax) + jnp.exp(g[3] - gmax)
        gw_ = 1.0 / gsum
        e = [lg[8 + EXPERTS_PER_GROUP * i:8 + EXPERTS_PER_GROUP * (i + 1), :] for i in range(N_GROUPS)]
        esel = jnp.where(gidx == 0, e[0], jnp.where(gidx == 1, e[1], jnp.where(gidx == 2, e[2], e[3])))
        ri = lax.broadcasted_iota(i32, (EXPERTS_PER_GROUP, tm), 0)
        top1 = jnp.max(esel, axis=0, keepdims=True)
        idx1 = jnp.min(jnp.where(esel == top1, ri, EXPERTS_PER_GROUP), axis=0, keepdims=True)
        rest = jnp.where(ri == idx1, -jnp.inf, esel)
        top2 = jnp.max(rest, axis=0, keepdims=True)
        idx2 = jnp.min(jnp.where(rest == top2, ri, EXPERTS_PER_GROUP), axis=0, keepdims=True)
        e2 = jnp.exp(top2 - top1)
        den = 1.0 + e2
        ex0 = gidx * EXPERTS_PER_GROUP + idx1
        ex1 = gidx * EXPERTS_PER_GROUP + idx2
        eid_ref[:, rs] = jnp.concatenate([ex0, ex1, jnp.zeros((6, tm), i32)], axis=0)
        ew8 = jnp.concatenate([(1.0 / den) * gw_, (e2 / den) * gw_, jnp.zeros((6, tm), f32)], axis=0)
        wcol_ref[rs, :] = ew8.T
        rows = lax.broadcasted_iota(i32, (N_EXPERTS, tm), 0)
        onehot = jnp.where(rows == ex0, 1.0, 0.0) + jnp.where(rows == ex1, 1.0, 0.0)
        cnt_ref[...] += _dot(onehot.astype(bf16), jnp.ones((tm, 128), bf16))


def _mixer_out(o, y, x, gate, wout, n2g, sc, sh, wr, rb, seq, tm):
    n = x.shape[0]
    tpb = seq // tm
    hw = N_HEADS * HEAD_PAD
    row = lambda i: (i, 0)
    fixed = lambda i: (0, 0)
    per_b = lambda i: (i // tpb, 0, 0)
    colblk = lambda i: (0, i)
    return pl.pallas_call(
        _mixer_out_kernel,
        out_shape=(jax.ShapeDtypeStruct((n, D_MODEL), f32),
                   jax.ShapeDtypeStruct((n, D_MODEL // 2), u32),
                   jax.ShapeDtypeStruct((8, n), i32),
                   jax.ShapeDtypeStruct((n, 8), f32),
                   jax.ShapeDtypeStruct((N_EXPERTS, 128), f32)),
        grid=(n // tm,),
        in_specs=[pl.BlockSpec((tm, hw), row),
                  pl.BlockSpec((tm, 3 * GROUP_WIDTH), row),
                  pl.BlockSpec((tm, D_MODEL), row),
                  pl.BlockSpec((1, 1, D_MODEL), per_b),
                  pl.BlockSpec((hw + 3 * GROUP_WIDTH, D_MODEL), fixed),
                  pl.BlockSpec((1, D_MODEL), fixed),
                  pl.BlockSpec((1, 1, D_MODEL), per_b),
                  pl.BlockSpec((1, 1, D_MODEL), per_b),
                  pl.BlockSpec((ROUTER_ROWS, D_MODEL), fixed),
                  pl.BlockSpec((ROUTER_ROWS, 1), fixed)],
        out_specs=(pl.BlockSpec((tm, D_MODEL), row),
                   pl.BlockSpec((tm, D_MODEL // 2), row),
                   pl.BlockSpec((8, tm), colblk),
                   pl.BlockSpec((tm, 8), row),
                   pl.BlockSpec((N_EXPERTS, 128), fixed)),
        compiler_params=_cparams(("arbitrary",)),
        name="mixer_out_router",
    )(o, y, x, gate, wout, n2g, sc, sh, wr, rb)


def _moe_rank_kernel(eid_ref, pstart_ref, u_ref, dest_ref, prefix):
    @pl.when(pl.program_id(0) == 0)
    def _():
        prefix[...] = jnp.zeros(prefix.shape, f32)

    tm = eid_ref.shape[1]
    rows = lax.broadcasted_iota(i32, (N_EXPERTS, tm), 0)
    oh0 = jnp.where(rows == eid_ref[0:1, :], 1.0, 0.0)
    oh1 = jnp.where(rows == eid_ref[1:2, :], 1.0, 0.0)
    u = u_ref[...]
    cs0 = _dot(oh0.astype(bf16), u)
    cs1 = _dot(oh1.astype(bf16), u)
    tot0 = cs0[:, tm - 1:tm]
    tot1 = cs1[:, tm - 1:tm]
    base = prefix[...] + pstart_ref[...]
    d0 = jnp.sum(oh0 * (base + cs0 - 1.0), axis=0, keepdims=True)
    d1 = jnp.sum(oh1 * (base + tot0 + cs1 - 1.0), axis=0, keepdims=True)
    prefix[...] = prefix[...] + tot0 + tot1
    dest_ref[...] = jnp.concatenate([d0.astype(i32), d1.astype(i32), jnp.zeros((6, tm), i32)], axis=0)


def _moe_rank(eid, pstart, tm):
    n = eid.shape[1]
    r = jnp.arange(tm)
    u = (r[:, None] <= r[None, :]).astype(bf16)
    return pl.pallas_call(
        _moe_rank_kernel,
        out_shape=jax.ShapeDtypeStruct((8, n), i32),
        grid=(n // tm,),
        in_specs=[pl.BlockSpec((8, tm), lambda i: (0, i)),
                  pl.BlockSpec((N_EXPERTS, 1), lambda i: (0, 0)),
                  pl.BlockSpec((tm, tm), lambda i: (0, 0))],
        out_specs=pl.BlockSpec((8, tm), lambda i: (0, i)),
        scratch_shapes=[pltpu.VMEM((N_EXPERTS, 1), f32)],
        compiler_params=_cparams(("arbitrary",)),
        name="moe_rank",
    )(eid, pstart, u)


def _row_dma(src, src_row, dst, dst_row, sem):
    return pltpu.make_async_copy(src.at[pl.ds(src_row, 1), :], dst.at[pl.ds(dst_row, 1), :], sem)


def _loop(lo, hi, fn, unroll=1):
    def body(r, c):
        fn(r)
        return c
    lax.fori_loop(lo, hi, body, 0, unroll=unroll)


def _ffn_kernel(be_ref, nu_ref, src_cur, src_nxt, w1_ref, w3_ref, w2_ref, h_hbm, ys_ref, xbuf, gsem, w1b, w3b, w2b):
    i = pl.program_id(0)
    nb = pl.num_programs(0)
    nu = nu_ref[0]
    slot = i % 2
    used = i < nu

    def row_copy(idx_ref, s, r):
        return _row_dma(h_hbm, idx_ref[0, 0, r], xbuf.at[s], r, gsem.at[s])

    def wait_rows(idx_ref, s):
        _loop(0, MOE_BLOCK, lambda r: row_copy(idx_ref, s, r).wait(), unroll=8)

    @pl.when(i == 0)
    def _():
        _loop(0, MOE_BLOCK, lambda r: row_copy(src_cur, 0, r).start(), unroll=8)

    changed = jnp.logical_or(i == 0, be_ref[i] != be_ref[jnp.maximum(i - 1, 0)])

    @pl.when(jnp.logical_and(used, changed))
    def _():
        w1b[...] = w1_ref[0, 0].astype(bf16)
        w3b[...] = w3_ref[0, 0].astype(bf16)
        w2b[...] = w2_ref[0, 0].astype(bf16)

    @pl.when(used)
    def _():
        wait_rows(src_cur, slot)
        packed = xbuf[slot]
        lo = pltpu.bitcast(packed << 16, f32)
        hi = pltpu.bitcast(packed & jnp.uint32(0xFFFF0000), f32)
        xb = jnp.concatenate([lo, hi], axis=1).astype(bf16)
        a = _dot(xb, w1b[...])
        b = _dot(xb, w3b[...])
        hb = (_silu(a) * b).astype(bf16)
        y = _dot(hb, w2b[...])
        for r in range(MOE_BLOCK):
            row_copy(src_nxt, 1 - slot, r).start(priority=r % 2)
        ys_ref[...] = y

    @pl.when(jnp.logical_and(used, i == nb - 1))
    def _():
        wait_rows(src_nxt, 1 - slot)

    @pl.when(jnp.logical_not(used))
    def _():
        @pl.when(i == nu)
        def _():
            wait_rows(src_cur, slot)

        ys_ref[...] = jnp.zeros(ys_ref.shape, f32)


def _moe_ffn(h2p, row_src, blk_expert, n_used, w1, w3, w2, layer):
    n_rows = row_src.shape[0]
    nb = n_rows // MOE_BLOCK
    src3 = row_src.reshape(nb, 1, MOE_BLOCK)
    smem_blk = lambda f: pl.BlockSpec((1, 1, MOE_BLOCK), f, memory_space=pltpu.SMEM)
    wblk = lambda i, be, nu: (layer, be[jnp.minimum(i, nu[0] - 1)], 0, 0)
    grid_spec = pltpu.PrefetchScalarGridSpec(
        num_scalar_prefetch=2,
        grid=(nb,),
        in_specs=[smem_blk(lambda i, be, nu: (i, 0, 0)),
                  smem_blk(lambda i, be, nu: (jnp.minimum(i + 1, nb - 1), 0, 0)),
                  pl.BlockSpec((1, 1, D_MODEL, D_EXPERT), wblk),
                  pl.BlockSpec((1, 1, D_MODEL, D_EXPERT), wblk),
                  pl.BlockSpec((1, 1, D_EXPERT, D_MODEL), wblk),
                  pl.BlockSpec(memory_space=pl.ANY)],
        out_specs=pl.BlockSpec((MOE_BLOCK, D_MODEL), lambda i, be, nu: (i, 0)),
        scratch_shapes=[pltpu.VMEM((2, MOE_BLOCK, D_MODEL // 2), u32),
                        pltpu.SemaphoreType.DMA((2,)),
                        pltpu.VMEM((D_MODEL, D_EXPERT), bf16),
                        pltpu.VMEM((D_MODEL, D_EXPERT), bf16),
                        pltpu.VMEM((D_EXPERT, D_MODEL), bf16)])
    return pl.pallas_call(
        _ffn_kernel,
        out_shape=jax.ShapeDtypeStruct((n_rows, D_MODEL), f32),
        grid_spec=grid_spec,
        compiler_params=_cparams(("arbitrary",)),
        name="moe_ffn",
    )(blk_expert, n_used, src3, src3, w1, w3, w2, h2p)


def _combine_kernel(d0c, d1c, d0n, d1n, x_ref, wcol_ref, gate_ref, ys_hbm, o_ref, g0, g1, sem):
    i = pl.program_id(0)
    nb = pl.num_programs(0)
    slot = i % 2
    tm = x_ref.shape[0]

    def copies(d0_ref, d1_ref, s, r):
        return (_row_dma(ys_hbm, d0_ref[0, 0, r], g0.at[s], r, sem.at[s]),
                _row_dma(ys_hbm, d1_ref[0, 0, r], g1.at[s], r, sem.at[s]))

    def start(d0_ref, d1_ref, s):
        def fn(r):
            a, b = copies(d0_ref, d1_ref, s, r)
            a.start(priority=0)
            b.start(priority=1)
        _loop(0, tm, fn, unroll=8)

    @pl.when(i == 0)
    def _():
        start(d0c, d1c, 0)

    @pl.when(i + 1 < nb)
    def _():
        start(d0n, d1n, 1 - slot)

    def wait(r):
        a, b = copies(d0c, d1c, slot, r)
        a.wait()
        b.wait()

    _loop(0, tm, wait, unroll=8)
    w = wcol_ref[...]
    y = w[:, 0:1] * g0[slot] + w[:, 1:2] * g1[slot]
    o_ref[...] = x_ref[...] + gate_ref[0] * y


def _combine(x1, wcol, gate, ys, d0, d1, seq, tm):
    n = x1.shape[0]
    nt = n // tm
    tpb = seq // tm
    cur = pl.BlockSpec((1, 1, tm), lambda i: (i, 0, 0), memory_space=pltpu.SMEM)
    nxt = pl.BlockSpec((1, 1, tm), lambda i: (jnp.minimum(i + 1, nt - 1), 0, 0), memory_space=pltpu.SMEM)
    return pl.pallas_call(
        _combine_kernel,
        out_shape=jax.ShapeDtypeStruct((n, D_MODEL), f32),
        grid=(nt,),
        in_specs=[cur, cur, nxt, nxt,
                  pl.BlockSpec((tm, D_MODEL), lambda i: (i, 0)),
                  pl.BlockSpec((tm, 8), lambda i: (i, 0)),
                  pl.BlockSpec((1, 1, D_MODEL), lambda i: (i // tpb, 0, 0)),
                  pl.BlockSpec(memory_space=pl.ANY)],
        out_specs=pl.BlockSpec((tm, D_MODEL), lambda i: (i, 0)),
        scratch_shapes=[pltpu.VMEM((2, tm, D_MODEL), f32), pltpu.VMEM((2, tm, D_MODEL), f32),
                        pltpu.SemaphoreType.DMA((2,))],
        compiler_params=_cparams(("arbitrary",)),
        name="moe_combine",
    )(d0, d1, d0, d1, x1, wcol, gate, ys)


def _moe(h2p, x1, eid, wcol, cnt, gate, w1, w3, w2, layer, seq, tm):
    n = h2p.shape[0]
    n_rows = 2 * n + N_EXPERTS * MOE_BLOCK
    nb = n_rows // MOE_BLOCK
    counts = cnt[:, 0].astype(i32)
    padded = (counts + MOE_BLOCK - 1) // MOE_BLOCK * MOE_BLOCK
    pend = jnp.cumsum(padded)
    pstart = pend - padded
    n_used = (pend[-1:] // MOE_BLOCK).astype(i32)
    blk_start = jnp.arange(nb, dtype=i32) * MOE_BLOCK
    blk_expert = jnp.minimum(jnp.sum(blk_start[:, None] >= pend[None, :], axis=1), N_EXPERTS - 1).astype(i32)
    dest = _moe_rank(eid, pstart.astype(f32)[:, None], min(512, n))
    by_dest = jnp.argsort(dest[0:2].reshape(-1)).astype(i32)
    first = jnp.cumsum(counts) - counts
    off = jnp.arange(n_rows, dtype=i32).reshape(nb, MOE_BLOCK) - pstart[blk_expert][:, None]
    valid = off < counts[blk_expert][:, None]
    pick = jnp.clip(first[blk_expert][:, None] + off, 0, 2 * n - 1)
    row_src = jnp.where(valid, by_dest[pick] % n, 0).astype(i32).reshape(n_rows)
    ys = _moe_ffn(h2p, row_src, blk_expert, n_used, w1, w3, w2, layer)
    d0 = dest[0].reshape(n // tm, 1, tm)
    d1 = dest[1].reshape(n // tm, 1, tm)
    return x1, wcol, gate, ys, d0, d1


def _pad_to(a, axis, size):
    pad = [(0, 0)] * a.ndim
    pad[axis] = (0, size - a.shape[axis])
    return jnp.pad(a, pad)


def _prep_layer_params(w_in, q_a_norm_g, w_uq, kv_a_norm_g, w_ukv, q_norm_g, k_norm_g, w_pool, pool_scale,
                       hgrn_lb_logits, hgrn_out_norm_g, conv_dw_w, w_out, router_group_w, router_group_b,
                       router_expert_w, router_expert_b):
    L = w_in.shape[0]
    zc = lambda n_: jnp.zeros((L, D_MODEL, n_), f32)
    win = jnp.concatenate([w_in[:, :, 352:], w_in[:, :, 0:192], zc(64), w_in[:, :, 192:320], zc(64),
                           w_in[:, :, 320:352], w_in[:, :, 320:336], zc(16)], axis=2).astype(bf16)
    qag = _pad_to(q_a_norm_g, 1, 256)[:, None, :]
    half = MLA_ROPE // 2
    x1 = slice(MLA_NOPE, MLA_NOPE + half)
    wuq = w_uq.reshape(L, MLA_Q_LORA, N_HEADS, MLA_QK)
    wuq = _pad_to(_pad_to(jnp.concatenate([wuq, wuq[..., x1]], axis=3), 3, HEAD_PAD), 1, 256)
    wuq = wuq.reshape(L, 256, N_HEADS * HEAD_PAD).astype(bf16)
    kvag = kv_a_norm_g[:, None, :]
    wkv = w_ukv.reshape(L, MLA_KV_LORA, N_HEADS, MLA_NOPE + MLA_V)
    wk = _pad_to(wkv[..., :MLA_NOPE], 3, HEAD_PAD).reshape(L, MLA_KV_LORA, N_HEADS * HEAD_PAD).astype(bf16)
    wv = _pad_to(wkv[..., MLA_NOPE:], 3, HEAD_PAD).reshape(L, MLA_KV_LORA, N_HEADS * HEAD_PAD).astype(bf16)
    qg = _pad_to(jnp.concatenate([q_norm_g, q_norm_g[:, x1]], axis=1), 1, HEAD_PAD)[:, None, :]
    kg = _pad_to(jnp.concatenate([k_norm_g, k_norm_g[:, x1]], axis=1), 1, HEAD_PAD)[:, None, :]
    wpool = jnp.zeros((L, GROUP_WIDTH, GROUP_WIDTH), f32)
    for g in range(len(POOL_WINDOWS)):
        wpool = wpool.at[:, 64 * g:64 * (g + 1), 64 * g:64 * (g + 1)].set(w_pool[:, g])
    wpool = wpool.astype(bf16)
    lb_cum = jnp.cumsum(jax.nn.softmax(hgrn_lb_logits.astype(f32), axis=0), axis=0)
    lb = lb_cum - lb_cum[0:1]
    loglb = jnp.log(lb)[:, None, :]
    l1mlb = jnp.log1p(-lb)[:, None, :]
    oml = (1.0 - lb)[:, None, :]
    og = jnp.tile(hgrn_out_norm_g, (1, N_HEADS))[:, None, :]
    dww = _pad_to(conv_dw_w, 1, 32)
    wo_attn = _pad_to(w_out[:, :GROUP_WIDTH].reshape(L, N_HEADS, MLA_V, D_MODEL), 2, HEAD_PAD)
    wout = jnp.concatenate([wo_attn.reshape(L, N_HEADS * HEAD_PAD, D_MODEL), w_out[:, GROUP_WIDTH:]], axis=1).astype(bf16)
    wr = jnp.concatenate([jnp.swapaxes(router_group_w, 1, 2), jnp.zeros((L, 4, D_MODEL), f32),
                          jnp.swapaxes(router_expert_w, 1, 2)], axis=1)
    rb = jnp.concatenate([router_group_b, jnp.zeros((L, 4), f32), router_expert_b], axis=1)[:, :, None]
    return dict(win=win, qag=qag, wuq=wuq, kvag=kvag, wk=wk, wv=wv, qg=qg, kg=kg, wpool=wpool,
                pscale=pool_scale[:, None, :], loglb=loglb, l1mlb=l1mlb, oml=oml, og=og, dww=dww,
                wout=wout, wr=wr, rb=rb)


def _rope_tabs(positions):
    half = MLA_ROPE // 2
    inv_freq = ROPE_THETA ** (-jnp.arange(half, dtype=f32) / half)
    ang = positions.astype(f32).reshape(-1)[:, None] * inv_freq
    cos = jnp.cos(ang)
    sin = jnp.sin(ang)
    n = cos.shape[0]
    ctab = jnp.concatenate([jnp.ones((n, MLA_NOPE), f32), cos, cos, jnp.zeros((n, 32), f32)], axis=1)
    stab = jnp.concatenate([jnp.zeros((n, MLA_NOPE), f32), -sin, sin, jnp.zeros((n, 32), f32)], axis=1)
    return ctab, stab


def kernel(x, c, positions, w_ada, b_ada, norm1_g, norm2_g, w_in, q_a_norm_g, w_uq, kv_a_norm_g, w_ukv, q_norm_g, k_norm_g, w_pool, pool_scale, hgrn_lb_logits, hgrn_out_norm_g, conv_dw_w, conv_dw_b, conv_ln_g, conv_ln_b, w_out, router_group_w, router_group_b, router_expert_w, router_expert_b, w1, w3, w2):
    B, S, D = x.shape
    L = w_in.shape[0]
    n = B * S
    tm = min(512, S)
    tq = min(512, S)
    tseq = min(128, S)

    p = _prep_layer_params(w_in, q_a_norm_g, w_uq, kv_a_norm_g, w_ukv, q_norm_g, k_norm_g, w_pool, pool_scale,
                           hgrn_lb_logits, hgrn_out_norm_g, conv_dw_w, w_out, router_group_w, router_group_b,
                           router_expert_w, router_expert_b)
    ctab, stab = _rope_tabs(positions)
    consts = _seqmix_consts(tseq)
    mod = _ada_mod(c, w_ada, b_ada)
    xf = x.reshape(n, D)
    hw = N_HEADS * HEAD_PAD
    pending = None
    for l in range(L):
        m6 = [mod[l, :, i * D:(i + 1) * D][:, None, :] for i in range(6)]
        sh1, sc1, g1, sh2, sc2, g2 = m6
        xf, zrest, q, k, v = _mixer_in(xf, sh1, sc1, norm1_g[l][None, :], p['win'][l], p['qag'][l], p['wuq'][l],
                                       p['kvag'][l], p['wk'][l], p['wv'][l], p['qg'][l], p['kg'][l],
                                       ctab, stab, S, tm, pending)
        o = _attention(q.reshape(B, S, hw), k.reshape(B, S, hw), v.reshape(B, S, hw), tq).reshape(n, hw)
        y = _seqmix(zrest, p['wpool'][l], p['pscale'][l], p['loglb'][l], p['l1mlb'][l], p['oml'][l], p['og'][l],
                    p['dww'][l], conv_dw_b[l][None, :], conv_ln_g[l][None, :], conv_ln_b[l][None, :],
                    consts, B, S, tseq)
        x1, h2, eid, wcol, cnt = _mixer_out(o, y, xf, g1, p['wout'][l], norm2_g[l][None, :], sc2, sh2,
                                            p['wr'][l], p['rb'][l], S, tm)
        pending = _moe(h2, x1, eid, wcol, cnt, g2, w1, w3, w2, l, S, tm)
    return _combine(*pending, S, tm).reshape(B, S, D)
```

```python
import functools
import math

import jax
import jax.numpy as jnp
from jax import lax
from jax.experimental import pallas as pl
from jax.experimental.pallas import tpu as pltpu

f32 = jnp.float32
bf16 = jnp.bfloat16
i32 = jnp.int32
u32 = jnp.uint32

D_MODEL = 1024
GROUP_WIDTH = 256
N_HEADS = 4
MLA_QK = 96
MLA_NOPE = 64
MLA_ROPE = 32
MLA_V = 64
MLA_Q_LORA = 192
MLA_KV_LORA = 128
HEAD_PAD = 128
HGRN_D = 64
SUB = 16
KEPT = SUB * (SUB // 2) + (SUB // 2) ** 2
CONV_K = 31
POOL_WINDOWS = (2, 4, 8, 16)
N_GROUPS = 4
EXPERTS_PER_GROUP = 8
N_EXPERTS = 32
D_EXPERT = 512
MOE_BLOCK = 256
EPS = 1e-6
ROPE_THETA = 10000.0

Z_REST = 7 * GROUP_WIDTH
Z_COLS = Z_REST + 256 + 128 + 128
ROUTER_ROWS = 40

VMEM_LIMIT = 56 * 1024 * 1024


def _cparams(sem):
    return pltpu.CompilerParams(dimension_semantics=sem, vmem_limit_bytes=VMEM_LIMIT)


def _nt(a, b):
    return lax.dot_general(a, b, (((1,), (1,)), ((), ())), preferred_element_type=f32)


def _tn(a, b):
    return lax.dot_general(a, b, (((0,), (0,)), ((), ())), preferred_element_type=f32)


def _dot(a, b):
    return jnp.dot(a, b, preferred_element_type=f32)


def _split3(x):
    hi = x.astype(bf16)
    r1 = x - hi.astype(f32)
    mid = r1.astype(bf16)
    lo = (r1 - mid.astype(f32)).astype(bf16)
    return hi, mid, lo


def _silu(x):
    return x * jax.nn.sigmoid(x)


def _ada_kernel(c_ref, w_ref, b_ref, o_ref):
    c = c_ref[...]
    o_ref[0] = _dot(_silu(c).astype(bf16), w_ref[0].astype(bf16)) + b_ref[0]


def _ada_mod(c, w_ada, b_ada):
    L = w_ada.shape[0]
    B = c.shape[0]
    n6 = w_ada.shape[2]
    tn = 1536
    c8 = jnp.zeros((8, D_MODEL), f32).at[:B].set(c)
    out = pl.pallas_call(
        _ada_kernel,
        out_shape=jax.ShapeDtypeStruct((L, 8, n6), f32),
        grid=(L, n6 // tn),
        in_specs=[pl.BlockSpec((8, D_MODEL), lambda l, j: (0, 0)),
                  pl.BlockSpec((1, D_MODEL, tn), lambda l, j: (l, 0, j)),
                  pl.BlockSpec((1, 1, tn), lambda l, j: (l, 0, j))],
        out_specs=pl.BlockSpec((1, 8, tn), lambda l, j: (l, 0, j)),
        compiler_params=_cparams(("arbitrary", "arbitrary")),
        name="ada_mod",
    )(c8, w_ada, b_ada.reshape(L, 1, n6))
    return out[:, :B]


def _head_norm_rope(t, g, ctab, stab):
    real = lax.broadcasted_iota(i32, (1, HEAD_PAD), 1) < MLA_QK
    segs = []
    for h in range(N_HEADS):
        seg = t[:, h * HEAD_PAD:(h + 1) * HEAD_PAD]
        ms = jnp.sum(jnp.where(real, seg * seg, 0.0), axis=-1, keepdims=True) * (1.0 / MLA_QK)
        tn_ = seg * lax.rsqrt(ms + EPS) * g
        segs.append(tn_ * ctab + pltpu.roll(tn_, HEAD_PAD - MLA_ROPE // 2, 1) * stab)
    return jnp.concatenate(segs, axis=1)


def _mixer_in_body(x, sh_ref, sc_ref, g1_ref, win_ref, qag_ref, wuq_ref, kvag_ref, wk_ref, wv_ref,
                   qg_ref, kg_ref, ctab_ref, stab_ref, zrest_ref, q_ref, k_ref, v_ref, before_last_store=None):
    ms = jnp.mean(x * x, axis=-1, keepdims=True)
    h = x * lax.rsqrt(ms + EPS) * g1_ref[...]
    h = h * (1.0 + sc_ref[0]) + sh_ref[0]
    hb = h.astype(bf16)
    zm = _dot(hb, win_ref[:, Z_REST:])
    cq = zm[:, 0:256]
    ckv = zm[:, 256:384]
    kr = zm[:, 384:512]
    ctab = ctab_ref[...]
    stab = stab_ref[...]

    cqn = cq * lax.rsqrt(jnp.sum(cq * cq, axis=-1, keepdims=True) * (1.0 / MLA_Q_LORA) + EPS) * qag_ref[...]
    q = _dot(cqn.astype(bf16), wuq_ref[...])
    q = _head_norm_rope(q, qg_ref[...], ctab, stab)
    q_ref[...] = (q * (MLA_QK ** -0.5 * math.log2(math.e))).astype(bf16)

    ckvn = ckv * lax.rsqrt(jnp.mean(ckv * ckv, axis=-1, keepdims=True) + EPS) * kvag_ref[...]
    ckvb = ckvn.astype(bf16)
    k = _dot(ckvb, wk_ref[...]) + jnp.concatenate([kr] * N_HEADS, axis=1)
    k = _head_norm_rope(k, kg_ref[...], ctab, stab)
    k_ref[...] = k.astype(bf16)
    lane = lax.broadcasted_iota(i32, (1, N_HEADS * HEAD_PAD), 1)
    ones_lane = jnp.where(lane % HEAD_PAD == MLA_V, 1.0, 0.0).astype(f32)
    v_ref[...] = (_dot(ckvb, wv_ref[...]) + ones_lane).astype(bf16)
    zrest = _dot(hb, win_ref[:, :Z_REST])
    if before_last_store is not None:
        before_last_store()
    zrest_ref[...] = zrest


def _mixer_in_kernel(x_ref, *refs):
    _mixer_in_body(x_ref[...], *refs)


def _mixer_in_moe_kernel(d0c, d1c, d0n, d1n, x1_ref, wcol_ref, gate_ref, ys_hbm, *refs):
    params, (xout_ref, zrest_ref, q_ref, k_ref, v_ref), (g0, g1, sem) = refs[:13], refs[13:18], refs[18:]
    i = pl.program_id(0)
    nt = pl.num_programs(0)
    slot = i % 2
    tm = x1_ref.shape[0]

    def copies(d0_ref, d1_ref, s, r):
        return (_row_dma(ys_hbm, d0_ref[0, 0, r], g0.at[s], r, sem.at[s]),
                _row_dma(ys_hbm, d1_ref[0, 0, r], g1.at[s], r, sem.at[s]))

    def start(d0_ref, d1_ref, s, r):
        a, b = copies(d0_ref, d1_ref, s, r)
        a.start(priority=0)
        b.start(priority=1)

    def wait(d0_ref, d1_ref, s, r):
        a, b = copies(d0_ref, d1_ref, s, r)
        a.wait()
        b.wait()

    @pl.when(i == 0)
    def _():
        _loop(0, tm, lambda r: start(d0c, d1c, 0, r), unroll=8)

    _loop(0, tm, lambda r: wait(d0c, d1c, slot, r), unroll=8)
    w = wcol_ref[...]
    x = x1_ref[...] + gate_ref[0] * (w[:, 0:1] * g0[slot] + w[:, 1:2] * g1[slot])
    xout_ref[...] = x

    def prefetch():
        for r in range(tm):
            start(d0n, d1n, 1 - slot, r)

    _mixer_in_body(x, *params, zrest_ref, q_ref, k_ref, v_ref, before_last_store=prefetch)

    @pl.when(i == nt - 1)
    def _():
        _loop(0, tm, lambda r: wait(d0n, d1n, 1 - slot, r), unroll=8)


def _mixer_in(x, sh, sc, g1, win, qag, wuq, kvag, wk, wv, qg, kg, ctab, stab, seq, tm, pending=None):
    n = x.shape[0] if pending is None else pending[0].shape[0]
    nt = n // tm
    tpb = seq // tm
    hw = N_HEADS * HEAD_PAD
    row = lambda i: (i, 0)
    fixed = lambda i: (0, 0)
    per_b = lambda i: (i // tpb, 0, 0)
    param_specs = [pl.BlockSpec((1, 1, D_MODEL), per_b),
                   pl.BlockSpec((1, 1, D_MODEL), per_b),
                   pl.BlockSpec((1, D_MODEL), fixed),
                   pl.BlockSpec((D_MODEL, Z_COLS), fixed),
                   pl.BlockSpec((1, 256), fixed),
                   pl.BlockSpec((256, hw), fixed),
                   pl.BlockSpec((1, MLA_KV_LORA), fixed),
                   pl.BlockSpec((MLA_KV_LORA, hw), fixed),
                   pl.BlockSpec((MLA_KV_LORA, hw), fixed),
                   pl.BlockSpec((1, HEAD_PAD), fixed),
                   pl.BlockSpec((1, HEAD_PAD), fixed),
                   pl.BlockSpec((tm, HEAD_PAD), row),
                   pl.BlockSpec((tm, HEAD_PAD), row)]
    params = (sh, sc, g1, win, qag, wuq, kvag, wk, wv, qg, kg, ctab, stab)
    out_shape = (jax.ShapeDtypeStruct((n, Z_REST), f32),
                 jax.ShapeDtypeStruct((n, hw), bf16),
                 jax.ShapeDtypeStruct((n, hw), bf16),
                 jax.ShapeDtypeStruct((n, hw), bf16))
    out_specs = (pl.BlockSpec((tm, Z_REST), row),
                 pl.BlockSpec((tm, hw), row),
                 pl.BlockSpec((tm, hw), row),
                 pl.BlockSpec((tm, hw), row))
    if pending is None:
        return (x,) + tuple(pl.pallas_call(
            _mixer_in_kernel,
            out_shape=out_shape,
            grid=(nt,),
            in_specs=[pl.BlockSpec((tm, D_MODEL), row)] + param_specs,
            out_specs=out_specs,
            compiler_params=_cparams(("arbitrary",)),
            name="mixer_in",
        )(x, *params))
    x1, wcol, gate, ys, d0, d1 = pending
    cur = pl.BlockSpec((1, 1, tm), lambda i: (i, 0, 0), memory_space=pltpu.SMEM)
    nxt = pl.BlockSpec((1, 1, tm), lambda i: (jnp.minimum(i + 1, nt - 1), 0, 0), memory_space=pltpu.SMEM)
    return pl.pallas_call(
        _mixer_in_moe_kernel,
        out_shape=(jax.ShapeDtypeStruct((n, D_MODEL), f32),) + out_shape,
        grid=(nt,),
        in_specs=[cur, cur, nxt, nxt,
                  pl.BlockSpec((tm, D_MODEL), row),
                  pl.BlockSpec((tm, 8), row),
                  pl.BlockSpec((1, 1, D_MODEL), per_b),
                  pl.BlockSpec(memory_space=pl.ANY)] + param_specs,
        out_specs=(pl.BlockSpec((tm, D_MODEL), row),) + out_specs,
        scratch_shapes=[pltpu.VMEM((2, tm, D_MODEL), f32), pltpu.VMEM((2, tm, D_MODEL), f32),
                        pltpu.SemaphoreType.DMA((2,))],
        compiler_params=_cparams(("arbitrary",)),
        name="mixer_in_moe",
    )(d0, d1, d0, d1, x1, wcol, gate, ys, *params)


def _attn_kernel(q_ref, k_ref, v_ref, o_ref, m_scr, acc_scr, *, tq):
    qi = pl.program_id(2)
    q = q_ref[0]
    m_scr[...] = jnp.full(m_scr.shape, -jnp.inf, f32)
    acc_scr[...] = jnp.zeros(acc_scr.shape, f32)
    def block(qrows, m_prev, acc, start, size, masked):
        kb = k_ref[0, pl.ds(start, size), :]
        vb = v_ref[0, pl.ds(start, size), :]
        s = _nt(qrows, kb)
        if masked:
            r = lax.broadcasted_iota(i32, s.shape, 0)
            c = lax.broadcasted_iota(i32, s.shape, 1)
            s = jnp.where(c <= r, s, -1e30)
        m_new = jnp.maximum(m_prev, jnp.max(s, axis=-1, keepdims=True))
        alpha = jnp.exp2(m_prev - m_new)
        p = jnp.exp2(s - jnp.concatenate([m_new] * (size // HEAD_PAD), axis=1))
        return m_new, alpha * acc + _dot(p.astype(bf16), vb)

    def steps(js):
        m, acc = m_scr[...], acc_scr[...]
        for j in js:
            m, acc = block(q, m, acc, pl.multiple_of(j * tq, tq), tq, False)
        m_scr[...] = m
        acc_scr[...] = acc

    def quad(jq, carry):
        steps(tuple(4 * jq + i for i in range(4)))
        return carry

    lax.fori_loop(0, qi // 4, quad, 0)
    base = (qi // 4) * 4

    @pl.when(qi % 4 >= 2)
    def _():
        steps((base, base + 1))

    @pl.when(qi % 2 == 1)
    def _():
        steps((qi - 1,))

    _, acc = block(q, m_scr[...], acc_scr[...], pl.multiple_of(qi * tq, tq), tq, True)
    o_ref[0] = (acc / acc[:, MLA_V:MLA_V + 1]).astype(bf16)


def _attention(q, k, v, tq):
    b, s, hw = q.shape
    qspec = pl.BlockSpec((1, tq, HEAD_PAD), lambda bi, h, qi: (bi, qi, h))
    kvspec = pl.BlockSpec((1, s, HEAD_PAD), lambda bi, h, qi: (bi, 0, h))
    return pl.pallas_call(
        functools.partial(_attn_kernel, tq=tq),
        out_shape=jax.ShapeDtypeStruct((b, s, hw), bf16),
        grid=(b, N_HEADS, s // tq),
        in_specs=[qspec, kvspec, kvspec],
        out_specs=qspec,
        scratch_shapes=[pltpu.VMEM((tq, HEAD_PAD), f32), pltpu.VMEM((tq, HEAD_PAD), f32)],
        compiler_params=_cparams(("arbitrary", "arbitrary", "arbitrary")),
        name="mla_attention",
    )(q, k, v)


def _seqmix_kernel(z_ref, wpool_ref, pscale_ref, loglb_ref, l1mlb_ref, oml_ref, og_ref, dww_ref, dwb_ref,
                   lng_ref, lnb_ref, le_ref, bones_ref, sel_ref, bdmask_ref, cmask_ref,
                   y_ref, pool_ext, conv_ext, st_scr, *, t):
    step = pl.program_id(1)
    gw = GROUP_WIDTH

    @pl.when(step == 0)
    def _():
        pool_ext[0:16, :] = jnp.zeros((16, gw), f32)
        conv_ext[0:32, :] = jnp.zeros((32, gw), f32)
        st_scr[...] = jnp.zeros(st_scr.shape, f32)

    u = z_ref[:, 0:gw]
    pool_ext[16:16 + t, :] = u
    ext = pool_ext[...]
    sums = {}
    acc = ext
    for w in (1, 2, 4, 8):
        acc = acc + pltpu.roll(acc, w, 0)
        sums[2 * w] = acc[16:16 + t]
    col = lax.broadcasted_iota(i32, (t, gw), 1)
    pos1 = (step * t + lax.broadcasted_iota(i32, (t, gw), 0) + 1).astype(f32)
    wsum = jnp.where(col < 64, sums[2], jnp.where(col < 128, sums[4], jnp.where(col < 192, sums[8], sums[16])))
    wlen = jnp.where(col < 64, 2.0, jnp.where(col < 128, 4.0, jnp.where(col < 192, 8.0, 16.0)))
    pooled = wsum / jnp.minimum(pos1, wlen) - u
    y_ref[:, 0:gw] = (_dot(pooled.astype(bf16), wpool_ref[...]) * pscale_ref[...]).astype(bf16)
    pool_ext[0:16, :] = pool_ext[t:t + 16, :]

    uc = z_ref[:, 5 * gw:6 * gw] * jax.nn.sigmoid(z_ref[:, 6 * gw:7 * gw])
    conv_ext[32:32 + t, :] = uc
    cacc = jnp.zeros((t, gw), f32) + dwb_ref[...]
    for b in range(8):
        n_a = len(range(b, CONV_K, 8))
        xb = conv_ext[pl.ds(32 - (CONV_K - 1) + b, t + 8 * (n_a - 1)), :]
        for a in range(n_a):
            j = 8 * a + b
            cacc = cacc + xb[8 * a:8 * a + t] * dww_ref[j:j + 1, :]
    mu = jnp.mean(cacc, axis=-1, keepdims=True)
    cen = cacc - mu
    var = jnp.mean(cen * cen, axis=-1, keepdims=True)
    cn = cen * lax.rsqrt(var + EPS) * lng_ref[...] + lnb_ref[...]
    y_ref[:, 2 * gw:3 * gw] = _silu(cn).astype(bf16)
    conv_ext[0:32, :] = conv_ext[t:t + 32, :]

    hq = z_ref[:, 1 * gw:2 * gw]
    hf = z_ref[:, 2 * gw:3 * gw]
    v = z_ref[:, 3 * gw:4 * gw]
    hg = z_ref[:, 4 * gw:5 * gw]
    q = _silu(hq)
    ls = jnp.minimum(hf, 0.0) - jnp.log1p(jnp.exp(-jnp.abs(hf)))
    x1 = loglb_ref[...]
    x2 = l1mlb_ref[...] + ls
    logf = jnp.maximum(x1, x2) + jnp.log1p(jnp.exp(-jnp.abs(x1 - x2)))
    kk = oml_ref[...] * jax.nn.sigmoid(-hf)
    le = le_ref[...]
    hi, mid, lo = _split3(logf)
    cs = (_dot(le, hi) + _dot(le, mid) + _dot(le, lo)) * math.log2(math.e)
    bl = cs[0:t]
    be = cs[t:2 * t]
    qt = (q * jnp.exp2(bl)).astype(bf16)
    kt = (kk * jnp.exp2(be - bl)).astype(bf16)
    dec = jnp.exp2(be)
    vb = v.astype(bf16)
    bones = bones_ref[...]
    sel = sel_ref[...]
    bdmask = bdmask_ref[...]
    cmask = cmask_ref[...] > 0.0
    half = SUB // 2
    hw2 = gw // 2

    def by_t(x):
        rep = lambda a, nt: jnp.broadcast_to(a[:, None, :], (nt, half, gw)).reshape(nt * half, gw)
        return jnp.concatenate([rep(x, SUB), rep(x[half:SUB], half)], axis=0)

    def by_s(x):
        tile = lambda a, nt: jnp.broadcast_to(a[None, :, :], (nt, half, gw)).reshape(nt * half, gw)
        return jnp.concatenate([tile(x[0:half], SUB), tile(x[half:SUB], half)], axis=0)

    st = [st_scr[0], st_scr[1]]
    outs = []
    for j in range(t // SUB):
        rows = slice(j * SUB, (j + 1) * SUB)
        diff = by_t(bl[rows]) - by_s(bl[rows])
        e = jnp.exp2(jnp.where(cmask, diff, -1e30))
        p = (by_t(q[rows]) * by_s(kk[rows]) * e).astype(bf16)
        a = _dot(p, bones)
        w = (a * by_s(v[rows])).astype(bf16)
        o_j = _dot(sel, w)
        o_inter = []
        for hp in range(2):
            cols = slice(hp * hw2, (hp + 1) * hw2)
            o_inter.append(_nt(qt[rows, cols], st[hp].astype(bf16)))
            st[hp] = st[hp] * dec[j * SUB:j * SUB + 1, cols] + bdmask * _tn(vb[rows, cols], kt[rows, cols])
        outs.append(o_j + jnp.concatenate(o_inter, axis=1))
    st_scr[0] = st[0]
    st_scr[1] = st[1]
    o = jnp.concatenate(outs, axis=0)
    o2 = o * o
    ohi = o2.astype(bf16)
    olo = (o2 - ohi.astype(f32)).astype(bf16)
    msq = (_dot(ohi, bones) + _dot(olo, bones)) * (1.0 / HGRN_D)
    on = o * lax.rsqrt(msq + EPS) * og_ref[...]
    y_ref[:, gw:2 * gw] = (on * _silu(hg)).astype(bf16)


def _seqmix_consts(t):
    r = jnp.arange(t)
    same = (r[:, None] // SUB) == (r[None, :] // SUB)
    ltri = same & (r[None, :] <= r[:, None])
    le = jnp.concatenate([ltri, same], axis=0).astype(bf16)
    c = jnp.arange(GROUP_WIDTH)
    bd = (c[:, None] // HGRN_D) == (c[None, :] // HGRN_D)
    bones = bd.astype(bf16)
    bdmask = bd[:GROUP_WIDTH // 2, :GROUP_WIDTH // 2].astype(f32)
    half = SUB // 2
    r = jnp.arange(KEPT)
    first = r < SUB * half
    t_of = jnp.where(first, r // half, half + (r - SUB * half) // half)
    s_of = jnp.where(first, r % half, half + r % half)
    sel = (t_of[None, :] == jnp.arange(SUB)[:, None]).astype(bf16)
    cmask = (s_of <= t_of).astype(f32)[:, None] * jnp.ones((1, GROUP_WIDTH), f32)
    return le, bones, sel, bdmask, cmask


def _seqmix(zrest, wpool, pscale, loglb, l1mlb, oml, og, dww, dwb, lng, lnb, consts, batch, seq, t):
    n = zrest.shape[0]
    spb = seq // t
    le, bones, sel, bdmask, cmask = consts
    row = lambda b, s: (b * spb + s, 0)
    fixed = lambda b, s: (0, 0)
    vec = pl.BlockSpec((1, GROUP_WIDTH), fixed)
    return pl.pallas_call(
        functools.partial(_seqmix_kernel, t=t),
        out_shape=jax.ShapeDtypeStruct((n, 3 * GROUP_WIDTH), bf16),
        grid=(batch, spb),
        in_specs=[pl.BlockSpec((t, Z_REST), row),
                  pl.BlockSpec((GROUP_WIDTH, GROUP_WIDTH), fixed),
                  vec, vec, vec, vec, vec,
                  pl.BlockSpec((32, GROUP_WIDTH), fixed),
                  vec, vec, vec,
                  pl.BlockSpec((2 * t, t), fixed),
                  pl.BlockSpec((GROUP_WIDTH, GROUP_WIDTH), fixed),
                  pl.BlockSpec((SUB, KEPT), fixed),
                  pl.BlockSpec((GROUP_WIDTH // 2, GROUP_WIDTH // 2), fixed),
                  pl.BlockSpec((KEPT, GROUP_WIDTH), fixed)],
        out_specs=pl.BlockSpec((t, 3 * GROUP_WIDTH), row),
        scratch_shapes=[pltpu.VMEM((t + 16, GROUP_WIDTH), f32),
                        pltpu.VMEM((t + 32, GROUP_WIDTH), f32),
                        pltpu.VMEM((2, GROUP_WIDTH // 2, GROUP_WIDTH // 2), f32)],
        compiler_params=_cparams(("arbitrary", "arbitrary")),
        name="seq_mixers",
    )(zrest, wpool, pscale, loglb, l1mlb, oml, og, dww, dwb, lng, lnb, le, bones, sel, bdmask, cmask)


def _mixer_out_kernel(o_ref, y_ref, x_ref, gate_ref, wout_ref, n2g_ref, sc_ref, sh_ref, wr_ref, rb_ref,
                      x1_ref, h2_ref, eid_ref, wcol_ref, cnt_ref):
    hw = N_HEADS * HEAD_PAD
    wr = wr_ref[...]
    whi = wr.astype(bf16)
    wlo = (wr - whi.astype(f32)).astype(bf16)

    @pl.when(pl.program_id(0) == 0)
    def _():
        cnt_ref[...] = jnp.zeros(cnt_ref.shape, f32)

    tm = x_ref.shape[0]
    for r0 in (0,):
        rs = slice(r0, r0 + tm)
        tmix = _dot(o_ref[rs, :], wout_ref[0:hw, :]) + _dot(y_ref[rs, :], wout_ref[hw:, :])
        x1 = x_ref[rs, :] + gate_ref[0] * tmix
        x1_ref[rs, :] = x1
        ms = jnp.mean(x1 * x1, axis=-1, keepdims=True)
        h2 = x1 * lax.rsqrt(ms + EPS) * n2g_ref[...]
        h2 = h2 * (1.0 + sc_ref[0]) + sh_ref[0]
        hhi = h2.astype(bf16)
        bits = pltpu.bitcast(hhi.astype(f32), u32)
        h2_ref[rs, :] = (bits[:, D_MODEL // 2:] & jnp.uint32(0xFFFF0000)) | (bits[:, :D_MODEL // 2] >> 16)
        hlo = (h2 - hhi.astype(f32)).astype(bf16)
        lg = _nt(whi, hhi) + _nt(whi, hlo) + _nt(wlo, hhi) + rb_ref[...]
        g = [lg[i:i + 1, :] for i in range(N_GROUPS)]
        gmax = jnp.maximum(jnp.maximum(g[0], g[1]), jnp.maximum(g[2], g[3]))
        gidx = jnp.where(g[0] == gmax, 0, jnp.where(g[1] == gmax, 1, jnp.where(g[2] == gmax, 2, 3)))
        gsum = jnp.exp(g[0] - gmax) + jnp.exp(g[1] - gmax) + jnp.exp(g[2] - gmax) + jnp.exp(g[3] - gmax)
        gw_ = 1.0 / gsum
        e = [lg[8 + EXPERTS_PER_GROUP * i:8 + EXPERTS_PER_GROUP * (i + 1), :] for i in range(N_GROUPS)]
        esel = jnp.where(gidx == 0, e[0], jnp.where(gidx == 1, e[1], jnp.where(gidx == 2, e[2], e[3])))
        ri = lax.broadcasted_iota(i32, (EXPERTS_PER_GROUP, tm), 0)
        top1 = jnp.max(esel, axis=0, keepdims=True)
        idx1 = jnp.min(jnp.where(esel == top1, ri, EXPERTS_PER_GROUP), axis=0, keepdims=True)
        rest = jnp.where(ri == idx1, -jnp.inf, esel)
        top2 = jnp.max(rest, axis=0, keepdims=True)
        idx2 = jnp.min(jnp.where(rest == top2, ri, EXPERTS_PER_GROUP), axis=0, keepdims=True)
        e2 = jnp.exp(top2 - top1)
        den = 1.0 + e2
        ex0 = gidx * EXPERTS_PER_GROUP + idx1
        ex1 = gidx * EXPERTS_PER_GROUP + idx2
        eid_ref[:, rs] = jnp.concatenate([ex0, ex1, jnp.zeros((6, tm), i32)], axis=0)
        ew8 = jnp.concatenate([(1.0 / den) * gw_, (e2 / den) * gw_, jnp.zeros((6, tm), f32)], axis=0)
        wcol_ref[rs, :] = ew8.T
        rows = lax.broadcasted_iota(i32, (N_EXPERTS, tm), 0)
        onehot = jnp.where(rows == ex0, 1.0, 0.0) + jnp.where(rows == ex1, 1.0, 0.0)
        cnt_ref[...] += _dot(onehot.astype(bf16), jnp.ones((tm, 128), bf16))


def _mixer_out(o, y, x, gate, wout, n2g, sc, sh, wr, rb, seq, tm):
    n = x.shape[0]
    tpb = seq // tm
    hw = N_HEADS * HEAD_PAD
    row = lambda i: (i, 0)
    fixed = lambda i: (0, 0)
    per_b = lambda i: (i // tpb, 0, 0)
    colblk = lambda i: (0, i)
    return pl.pallas_call(
        _mixer_out_kernel,
        out_shape=(jax.ShapeDtypeStruct((n, D_MODEL), f32),
                   jax.ShapeDtypeStruct((n, D_MODEL // 2), u32),
                   jax.ShapeDtypeStruct((8, n), i32),
                   jax.ShapeDtypeStruct((n, 8), f32),
                   jax.ShapeDtypeStruct((N_EXPERTS, 128), f32)),
        grid=(n // tm,),
        in_specs=[pl.BlockSpec((tm, hw), row),
                  pl.BlockSpec((tm, 3 * GROUP_WIDTH), row),
                  pl.BlockSpec((tm, D_MODEL), row),
                  pl.BlockSpec((1, 1, D_MODEL), per_b),
                  pl.BlockSpec((hw + 3 * GROUP_WIDTH, D_MODEL), fixed),
                  pl.BlockSpec((1, D_MODEL), fixed),
                  pl.BlockSpec((1, 1, D_MODEL), per_b),
                  pl.BlockSpec((1, 1, D_MODEL), per_b),
                  pl.BlockSpec((ROUTER_ROWS, D_MODEL), fixed),
                  pl.BlockSpec((ROUTER_ROWS, 1), fixed)],
        out_specs=(pl.BlockSpec((tm, D_MODEL), row),
                   pl.BlockSpec((tm, D_MODEL // 2), row),
                   pl.BlockSpec((8, tm), colblk),
                   pl.BlockSpec((tm, 8), row),
                   pl.BlockSpec((N_EXPERTS, 128), fixed)),
        compiler_params=_cparams(("arbitrary",)),
        name="mixer_out_router",
    )(o, y, x, gate, wout, n2g, sc, sh, wr, rb)


def _moe_rank_kernel(eid_ref, pstart_ref, u_ref, dest_ref, prefix):
    @pl.when(pl.program_id(0) == 0)
    def _():
        prefix[...] = jnp.zeros(prefix.shape, f32)

    tm = eid_ref.shape[1]
    rows = lax.broadcasted_iota(i32, (N_EXPERTS, tm), 0)
    oh0 = jnp.where(rows == eid_ref[0:1, :], 1.0, 0.0)
    oh1 = jnp.where(rows == eid_ref[1:2, :], 1.0, 0.0)
    u = u_ref[...]
    cs0 = _dot(oh0.astype(bf16), u)
    cs1 = _dot(oh1.astype(bf16), u)
    tot0 = cs0[:, tm - 1:tm]
    tot1 = cs1[:, tm - 1:tm]
    base = prefix[...] + pstart_ref[...]
    d0 = jnp.sum(oh0 * (base + cs0 - 1.0), axis=0, keepdims=True)
    d1 = jnp.sum(oh1 * (base + tot0 + cs1 - 1.0), axis=0, keepdims=True)
    prefix[...] = prefix[...] + tot0 + tot1
    dest_ref[...] = jnp.concatenate([d0.astype(i32), d1.astype(i32), jnp.zeros((6, tm), i32)], axis=0)


def _moe_rank(eid, pstart, tm):
    n = eid.shape[1]
    r = jnp.arange(tm)
    u = (r[:, None] <= r[None, :]).astype(bf16)
    return pl.pallas_call(
        _moe_rank_kernel,
        out_shape=jax.ShapeDtypeStruct((8, n), i32),
        grid=(n // tm,),
        in_specs=[pl.BlockSpec((8, tm), lambda i: (0, i)),
                  pl.BlockSpec((N_EXPERTS, 1), lambda i: (0, 0)),
                  pl.BlockSpec((tm, tm), lambda i: (0, 0))],
        out_specs=pl.BlockSpec((8, tm), lambda i: (0, i)),
        scratch_shapes=[pltpu.VMEM((N_EXPERTS, 1), f32)],
        compiler_params=_cparams(("arbitrary",)),
        name="moe_rank",
    )(eid, pstart, u)


def _row_dma(src, src_row, dst, dst_row, sem):
    return pltpu.make_async_copy(src.at[pl.ds(src_row, 1), :], dst.at[pl.ds(dst_row, 1), :], sem)


def _loop(lo, hi, fn, unroll=1):
    def body(r, c):
        fn(r)
        return c
    lax.fori_loop(lo, hi, body, 0, unroll=unroll)


FFN_SLOTS = 3


def _ffn_kernel(be_ref, nu_ref, src0, src1, src2, w1_ref, w3_ref, w2_ref, h_hbm, ys_ref, xbuf, gsem,
                w1b, w3b, w2b):
    i = pl.program_id(0)
    nb = pl.num_programs(0)
    nu = nu_ref[0]
    used = i < nu

    slot0 = i % FFN_SLOTS
    slot1 = (i + 1) % FFN_SLOTS
    slot2 = (i + 2) % FFN_SLOTS

    def row_copy(idx_ref, s, r):
        return _row_dma(h_hbm, idx_ref[0, 0, r], xbuf.at[s], r, gsem.at[s])

    def start_rows(idx_ref, s):
        _loop(0, MOE_BLOCK, lambda r: row_copy(idx_ref, s, r).start(), unroll=8)

    def wait_rows(idx_ref, s):
        _loop(0, MOE_BLOCK, lambda r: row_copy(idx_ref, s, r).wait(), unroll=8)

    @pl.when(i == 0)
    def _():
        start_rows(src0, 0)
        start_rows(src1, 1)

    changed = jnp.logical_or(i == 0, be_ref[i] != be_ref[jnp.maximum(i - 1, 0)])

    @pl.when(jnp.logical_and(used, changed))
    def _():
        w1b[...] = w1_ref[0, 0].astype(bf16)
        w3b[...] = w3_ref[0, 0].astype(bf16)
        w2b[...] = w2_ref[0, 0].astype(bf16)

    @pl.when(used)
    def _():
        wait_rows(src0, slot0)
        packed = xbuf[slot0]
        lo = pltpu.bitcast(packed << 16, f32)
        hi = pltpu.bitcast(packed & jnp.uint32(0xFFFF0000), f32)
        xb = jnp.concatenate([lo, hi], axis=1).astype(bf16)
        a = _dot(xb, w1b[...])
        b = _dot(xb, w3b[...])
        hb = (_silu(a) * b).astype(bf16)
        y = _dot(hb, w2b[...])
        for r in range(MOE_BLOCK):
            row_copy(src2, slot2, r).start(priority=r % 2)
        ys_ref[...] = y

    @pl.when(jnp.logical_not(used))
    def _():
        @pl.when(i < nu + 2)
        def _():
            wait_rows(src0, slot0)

        ys_ref[...] = jnp.zeros(ys_ref.shape, f32)

    @pl.when(i == nb - 1)
    def _():
        @pl.when(i - 1 < nu)
        def _():
            wait_rows(src1, slot1)

        @pl.when(used)
        def _():
            wait_rows(src2, slot2)


def _moe_ffn(h2p, row_src, blk_expert, n_used, w1, w3, w2, layer):
    n_rows = row_src.shape[0]
    nb = n_rows // MOE_BLOCK
    src3 = jnp.concatenate([row_src, jnp.zeros((2 * MOE_BLOCK,), i32)]).reshape(nb + 2, 1, MOE_BLOCK)
    smem_blk = lambda ahead: pl.BlockSpec((1, 1, MOE_BLOCK), lambda i, be, nu: (i + ahead, 0, 0),
                                          memory_space=pltpu.SMEM)
    wblk = lambda i, be, nu: (layer, be[jnp.minimum(i, nu[0] - 1)], 0, 0)
    grid_spec = pltpu.PrefetchScalarGridSpec(
        num_scalar_prefetch=2,
        grid=(nb,),
        in_specs=[smem_blk(0), smem_blk(1), smem_blk(2),
                  pl.BlockSpec((1, 1, D_MODEL, D_EXPERT), wblk),
                  pl.BlockSpec((1, 1, D_MODEL, D_EXPERT), wblk),
                  pl.BlockSpec((1, 1, D_EXPERT, D_MODEL), wblk),
                  pl.BlockSpec(memory_space=pl.ANY)],
        out_specs=pl.BlockSpec((MOE_BLOCK, D_MODEL), lambda i, be, nu: (i, 0)),
        scratch_shapes=[pltpu.VMEM((FFN_SLOTS, MOE_BLOCK, D_MODEL // 2), u32),
                        pltpu.SemaphoreType.DMA((FFN_SLOTS,)),
                        pltpu.VMEM((D_MODEL, D_EXPERT), bf16),
                        pltpu.VMEM((D_MODEL, D_EXPERT), bf16),
                        pltpu.VMEM((D_EXPERT, D_MODEL), bf16)])
    return pl.pallas_call(
        _ffn_kernel,
        out_shape=jax.ShapeDtypeStruct((n_rows, D_MODEL), f32),
        grid_spec=grid_spec,
        compiler_params=_cparams(("arbitrary",)),
        name="moe_ffn",
    )(blk_expert, n_used, src3, src3, src3, w1, w3, w2, h2p)


def _combine_kernel(d0c, d1c, d0n, d1n, x_ref, wcol_ref, gate_ref, ys_hbm, o_ref, g0, g1, sem):
    i = pl.program_id(0)
    nb = pl.num_programs(0)
    slot = i % 2
    tm = x_ref.shape[0]

    def copies(d0_ref, d1_ref, s, r):
        return (_row_dma(ys_hbm, d0_ref[0, 0, r], g0.at[s], r, sem.at[s]),
                _row_dma(ys_hbm, d1_ref[0, 0, r], g1.at[s], r, sem.at[s]))

    def start(d0_ref, d1_ref, s):
        def fn(r):
            a, b = copies(d0_ref, d1_ref, s, r)
            a.start(priority=0)
            b.start(priority=1)
        _loop(0, tm, fn, unroll=8)

    @pl.when(i == 0)
    def _():
        start(d0c, d1c, 0)

    @pl.when(i + 1 < nb)
    def _():
        start(d0n, d1n, 1 - slot)

    def wait(r):
        a, b = copies(d0c, d1c, slot, r)
        a.wait()
        b.wait()

    _loop(0, tm, wait, unroll=8)
    w = wcol_ref[...]
    y = w[:, 0:1] * g0[slot] + w[:, 1:2] * g1[slot]
    o_ref[...] = x_ref[...] + gate_ref[0] * y


def _combine(x1, wcol, gate, ys, d0, d1, seq, tm):
    n = x1.shape[0]
    nt = n // tm
    tpb = seq // tm
    cur = pl.BlockSpec((1, 1, tm), lambda i: (i, 0, 0), memory_space=pltpu.SMEM)
    nxt = pl.BlockSpec((1, 1, tm), lambda i: (jnp.minimum(i + 1, nt - 1), 0, 0), memory_space=pltpu.SMEM)
    return pl.pallas_call(
        _combine_kernel,
        out_shape=jax.ShapeDtypeStruct((n, D_MODEL), f32),
        grid=(nt,),
        in_specs=[cur, cur, nxt, nxt,
                  pl.BlockSpec((tm, D_MODEL), lambda i: (i, 0)),
                  pl.BlockSpec((tm, 8), lambda i: (i, 0)),
                  pl.BlockSpec((1, 1, D_MODEL), lambda i: (i // tpb, 0, 0)),
                  pl.BlockSpec(memory_space=pl.ANY)],
        out_specs=pl.BlockSpec((tm, D_MODEL), lambda i: (i, 0)),
        scratch_shapes=[pltpu.VMEM((2, tm, D_MODEL), f32), pltpu.VMEM((2, tm, D_MODEL), f32),
                        pltpu.SemaphoreType.DMA((2,))],
        compiler_params=_cparams(("arbitrary",)),
        name="moe_combine",
    )(d0, d1, d0, d1, x1, wcol, gate, ys)


def _moe(h2p, x1, eid, wcol, cnt, gate, w1, w3, w2, layer, seq, tm):
    n = h2p.shape[0]
    n_rows = 2 * n + N_EXPERTS * MOE_BLOCK
    nb = n_rows // MOE_BLOCK
    counts = cnt[:, 0].astype(i32)
    padded = (counts + MOE_BLOCK - 1) // MOE_BLOCK * MOE_BLOCK
    pend = jnp.cumsum(padded)
    pstart = pend - padded
    n_used = (pend[-1:] // MOE_BLOCK).astype(i32)
    blk_start = jnp.arange(nb, dtype=i32) * MOE_BLOCK
    blk_expert = jnp.minimum(jnp.sum(blk_start[:, None] >= pend[None, :], axis=1), N_EXPERTS - 1).astype(i32)
    dest = _moe_rank(eid, pstart.astype(f32)[:, None], min(512, n))
    by_dest = jnp.argsort(dest[0:2].reshape(-1)).astype(i32)
    first = jnp.cumsum(counts) - counts
    off = jnp.arange(n_rows, dtype=i32).reshape(nb, MOE_BLOCK) - pstart[blk_expert][:, None]
    valid = off < counts[blk_expert][:, None]
    pick = jnp.clip(first[blk_expert][:, None] + off, 0, 2 * n - 1)
    row_src = jnp.where(valid, by_dest[pick] % n, 0).astype(i32).reshape(n_rows)
    ys = _moe_ffn(h2p, row_src, blk_expert, n_used, w1, w3, w2, layer)
    d0 = dest[0].reshape(n // tm, 1, tm)
    d1 = dest[1].reshape(n // tm, 1, tm)
    return x1, wcol, gate, ys, d0, d1


def _pad_to(a, axis, size):
    pad = [(0, 0)] * a.ndim
    pad[axis] = (0, size - a.shape[axis])
    return jnp.pad(a, pad)


def _prep_layer_params(w_in, q_a_norm_g, w_uq, kv_a_norm_g, w_ukv, q_norm_g, k_norm_g, w_pool, pool_scale,
                       hgrn_lb_logits, hgrn_out_norm_g, conv_dw_w, w_out, router_group_w, router_group_b,
                       router_expert_w, router_expert_b):
    L = w_in.shape[0]
    zc = lambda n_: jnp.zeros((L, D_MODEL, n_), f32)
    win = jnp.concatenate([w_in[:, :, 352:], w_in[:, :, 0:192], zc(64), w_in[:, :, 192:320], zc(64),
                           w_in[:, :, 320:352], w_in[:, :, 320:336], zc(16)], axis=2).astype(bf16)
    qag = _pad_to(q_a_norm_g, 1, 256)[:, None, :]
    half = MLA_ROPE // 2
    x1 = slice(MLA_NOPE, MLA_NOPE + half)
    wuq = w_uq.reshape(L, MLA_Q_LORA, N_HEADS, MLA_QK)
    wuq = _pad_to(_pad_to(jnp.concatenate([wuq, wuq[..., x1]], axis=3), 3, HEAD_PAD), 1, 256)
    wuq = wuq.reshape(L, 256, N_HEADS * HEAD_PAD).astype(bf16)
    kvag = kv_a_norm_g[:, None, :]
    wkv = w_ukv.reshape(L, MLA_KV_LORA, N_HEADS, MLA_NOPE + MLA_V)
    wk = _pad_to(wkv[..., :MLA_NOPE], 3, HEAD_PAD).reshape(L, MLA_KV_LORA, N_HEADS * HEAD_PAD).astype(bf16)
    wv = _pad_to(wkv[..., MLA_NOPE:], 3, HEAD_PAD).reshape(L, MLA_KV_LORA, N_HEADS * HEAD_PAD).astype(bf16)
    qg = _pad_to(jnp.concatenate([q_norm_g, q_norm_g[:, x1]], axis=1), 1, HEAD_PAD)[:, None, :]
    kg = _pad_to(jnp.concatenate([k_norm_g, k_norm_g[:, x1]], axis=1), 1, HEAD_PAD)[:, None, :]
    wpool = jnp.zeros((L, GROUP_WIDTH, GROUP_WIDTH), f32)
    for g in range(len(POOL_WINDOWS)):
        wpool = wpool.at[:, 64 * g:64 * (g + 1), 64 * g:64 * (g + 1)].set(w_pool[:, g])
    wpool = wpool.astype(bf16)
    lb_cum = jnp.cumsum(jax.nn.softmax(hgrn_lb_logits.astype(f32), axis=0), axis=0)
    lb = lb_cum - lb_cum[0:1]
    loglb = jnp.log(lb)[:, None, :]
    l1mlb = jnp.log1p(-lb)[:, None, :]
    oml = (1.0 - lb)[:, None, :]
    og = jnp.tile(hgrn_out_norm_g, (1, N_HEADS))[:, None, :]
    dww = _pad_to(conv_dw_w, 1, 32)
    wo_attn = _pad_to(w_out[:, :GROUP_WIDTH].reshape(L, N_HEADS, MLA_V, D_MODEL), 2, HEAD_PAD)
    wout = jnp.concatenate([wo_attn.reshape(L, N_HEADS * HEAD_PAD, D_MODEL), w_out[:, GROUP_WIDTH:]], axis=1).astype(bf16)
    wr = jnp.concatenate([jnp.swapaxes(router_group_w, 1, 2), jnp.zeros((L, 4, D_MODEL), f32),
                          jnp.swapaxes(router_expert_w, 1, 2)], axis=1)
    rb = jnp.concatenate([router_group_b, jnp.zeros((L, 4), f32), router_expert_b], axis=1)[:, :, None]
    return dict(win=win, qag=qag, wuq=wuq, kvag=kvag, wk=wk, wv=wv, qg=qg, kg=kg, wpool=wpool,
                pscale=pool_scale[:, None, :], loglb=loglb, l1mlb=l1mlb, oml=oml, og=og, dww=dww,
                wout=wout, wr=wr, rb=rb)


def _rope_tabs(positions):
    half = MLA_ROPE // 2
    inv_freq = ROPE_THETA ** (-jnp.arange(half, dtype=f32) / half)
    ang = positions.astype(f32).reshape(-1)[:, None] * inv_freq
    cos = jnp.cos(ang)
    sin = jnp.sin(ang)
    n = cos.shape[0]
    ctab = jnp.concatenate([jnp.ones((n, MLA_NOPE), f32), cos, cos, jnp.zeros((n, 32), f32)], axis=1)
    stab = jnp.concatenate([jnp.zeros((n, MLA_NOPE), f32), -sin, sin, jnp.zeros((n, 32), f32)], axis=1)
    return ctab, stab


def kernel(x, c, positions, w_ada, b_ada, norm1_g, norm2_g, w_in, q_a_norm_g, w_uq, kv_a_norm_g, w_ukv, q_norm_g, k_norm_g, w_pool, pool_scale, hgrn_lb_logits, hgrn_out_norm_g, conv_dw_w, conv_dw_b, conv_ln_g, conv_ln_b, w_out, router_group_w, router_group_b, router_expert_w, router_expert_b, w1, w3, w2):
    B, S, D = x.shape
    L = w_in.shape[0]
    n = B * S
    tm = min(512, S)
    tq = min(512, S)
    tseq = min(256, S)

    p = _prep_layer_params(w_in, q_a_norm_g, w_uq, kv_a_norm_g, w_ukv, q_norm_g, k_norm_g, w_pool, pool_scale,
                           hgrn_lb_logits, hgrn_out_norm_g, conv_dw_w, w_out, router_group_w, router_group_b,
                           router_expert_w, router_expert_b)
    ctab, stab = _rope_tabs(positions)
    consts = _seqmix_consts(tseq)
    mod = _ada_mod(c, w_ada, b_ada)
    xf = x.reshape(n, D)
    hw = N_HEADS * HEAD_PAD
    pending = None
    for l in range(L):
        m6 = [mod[l, :, i * D:(i + 1) * D][:, None, :] for i in range(6)]
        sh1, sc1, g1, sh2, sc2, g2 = m6
        xf, zrest, q, k, v = _mixer_in(xf, sh1, sc1, norm1_g[l][None, :], p['win'][l], p['qag'][l], p['wuq'][l],
                                       p['kvag'][l], p['wk'][l], p['wv'][l], p['qg'][l], p['kg'][l],
                                       ctab, stab, S, tm, pending)
        o = _attention(q.reshape(B, S, hw), k.reshape(B, S, hw), v.reshape(B, S, hw), tq).reshape(n, hw)
        y = _seqmix(zrest, p['wpool'][l], p['pscale'][l], p['loglb'][l], p['l1mlb'][l], p['oml'][l], p['og'][l],
                    p['dww'][l], conv_dw_b[l][None, :], conv_ln_g[l][None, :], conv_ln_b[l][None, :],
                    consts, B, S, tseq)
        x1, h2, eid, wcol, cnt = _mixer_out(o, y, xf, g1, p['wout'][l], norm2_g[l][None, :], sc2, sh2,
                                            p['wr'][l], p['rb'][l], S, tm)
        pending = _moe(h2, x1, eid, wcol, cnt, g2, w1, w3, w2, l, S, tm)
    return _combine(*pending, S, tm).reshape(B, S, D)
```

```python
import functools
import math

import jax
import jax.numpy as jnp
from jax import lax
from jax.experimental import pallas as pl
from jax.experimental.pallas import tpu as pltpu

f32 = jnp.float32
bf16 = jnp.bfloat16
i32 = jnp.int32
u32 = jnp.uint32

D_MODEL = 1024
GROUP_WIDTH = 256
N_HEADS = 4
MLA_QK = 96
MLA_NOPE = 64
MLA_ROPE = 32
MLA_V = 64
MLA_Q_LORA = 192
MLA_KV_LORA = 128
HEAD_PAD = 128
HGRN_D = 64
SUB = 16
KEPT = SUB * (SUB // 2) + (SUB // 2) ** 2
CONV_K = 31
POOL_WINDOWS = (2, 4, 8, 16)
N_GROUPS = 4
EXPERTS_PER_GROUP = 8
N_EXPERTS = 32
D_EXPERT = 512
MOE_BLOCK = 256
EPS = 1e-6
ROPE_THETA = 10000.0

Z_REST = 7 * GROUP_WIDTH
Z_COLS = Z_REST + 256 + 128 + 128
ROUTER_ROWS = 40

V7X_VMEM_BYTES = 64 * 1024 * 1024
VMEM_LIMIT = V7X_VMEM_BYTES * 7 // 8

ROW_TILE = 512
ATTN_TILE = 1024
SEQ_TILE = 256


def _tiles(seq):
    return min(ROW_TILE, seq), min(ATTN_TILE, seq), min(SEQ_TILE, seq)


def _cparams(sem):
    return pltpu.CompilerParams(dimension_semantics=sem, vmem_limit_bytes=VMEM_LIMIT)


def _nt(a, b):
    return lax.dot_general(a, b, (((1,), (1,)), ((), ())), preferred_element_type=f32)


def _tn(a, b):
    return lax.dot_general(a, b, (((0,), (0,)), ((), ())), preferred_element_type=f32)


def _dot(a, b):
    return jnp.dot(a, b, preferred_element_type=f32)


def _split3(x):
    hi = x.astype(bf16)
    r1 = x - hi.astype(f32)
    mid = r1.astype(bf16)
    lo = (r1 - mid.astype(f32)).astype(bf16)
    return hi, mid, lo


def _silu(x):
    return x * jax.nn.sigmoid(x)


def _ada_kernel(c_ref, w_ref, b_ref, o_ref):
    c = c_ref[...]
    o_ref[0] = _dot(_silu(c).astype(bf16), w_ref[0].astype(bf16)) + b_ref[0]


def _ada_mod(c, w_ada, b_ada):
    L = w_ada.shape[0]
    B = c.shape[0]
    n6 = w_ada.shape[2]
    tn = 1536
    c8 = jnp.zeros((8, D_MODEL), f32).at[:B].set(c)
    out = pl.pallas_call(
        _ada_kernel,
        out_shape=jax.ShapeDtypeStruct((L, 8, n6), f32),
        grid=(L, n6 // tn),
        in_specs=[pl.BlockSpec((8, D_MODEL), lambda l, j: (0, 0)),
                  pl.BlockSpec((1, D_MODEL, tn), lambda l, j: (l, 0, j)),
                  pl.BlockSpec((1, 1, tn), lambda l, j: (l, 0, j))],
        out_specs=pl.BlockSpec((1, 8, tn), lambda l, j: (l, 0, j)),
        compiler_params=_cparams(("arbitrary", "arbitrary")),
        name="ada_mod",
    )(c8, w_ada, b_ada.reshape(L, 1, n6))
    return out[:, :B]


def _head_norm_rope(t, g, ctab, stab):
    real = lax.broadcasted_iota(i32, (1, HEAD_PAD), 1) < MLA_QK
    segs = []
    for h in range(N_HEADS):
        seg = t[:, h * HEAD_PAD:(h + 1) * HEAD_PAD]
        ms = jnp.sum(jnp.where(real, seg * seg, 0.0), axis=-1, keepdims=True) * (1.0 / MLA_QK)
        tn_ = seg * lax.rsqrt(ms + EPS) * g
        segs.append(tn_ * ctab + pltpu.roll(tn_, HEAD_PAD - MLA_ROPE // 2, 1) * stab)
    return jnp.concatenate(segs, axis=1)


def _mixer_in_body(x, sh_ref, sc_ref, g1_ref, win_ref, qag_ref, wuq_ref, kvag_ref, wk_ref, wv_ref,
                   qg_ref, kg_ref, ctab_ref, stab_ref, zrest_ref, q_ref, k_ref, v_ref, before_last_store=None):
    ms = jnp.mean(x * x, axis=-1, keepdims=True)
    h = x * lax.rsqrt(ms + EPS) * g1_ref[...]
    h = h * (1.0 + sc_ref[0]) + sh_ref[0]
    hb = h.astype(bf16)
    zm = _dot(hb, win_ref[:, Z_REST:])
    cq = zm[:, 0:256]
    ckv = zm[:, 256:384]
    kr = zm[:, 384:512]
    ctab = ctab_ref[...]
    stab = stab_ref[...]

    cqn = cq * lax.rsqrt(jnp.sum(cq * cq, axis=-1, keepdims=True) * (1.0 / MLA_Q_LORA) + EPS) * qag_ref[...]
    q = _dot(cqn.astype(bf16), wuq_ref[...])
    q = _head_norm_rope(q, qg_ref[...], ctab, stab)
    q_ref[...] = (q * (MLA_QK ** -0.5 * math.log2(math.e))).astype(bf16)

    ckvn = ckv * lax.rsqrt(jnp.mean(ckv * ckv, axis=-1, keepdims=True) + EPS) * kvag_ref[...]
    ckvb = ckvn.astype(bf16)
    k = _dot(ckvb, wk_ref[...]) + jnp.concatenate([kr] * N_HEADS, axis=1)
    k = _head_norm_rope(k, kg_ref[...], ctab, stab)
    k_ref[...] = k.astype(bf16)
    lane = lax.broadcasted_iota(i32, (1, N_HEADS * HEAD_PAD), 1)
    ones_lane = jnp.where(lane % HEAD_PAD == MLA_V, 1.0, 0.0).astype(f32)
    v_ref[...] = (_dot(ckvb, wv_ref[...]) + ones_lane).astype(bf16)
    zrest = _dot(hb, win_ref[:, :Z_REST])
    if before_last_store is not None:
        before_last_store()
    zrest_ref[...] = zrest


def _mixer_in_kernel(x_ref, *refs):
    _mixer_in_body(x_ref[...], *refs)


def _mixer_in_moe_kernel(d0c, d1c, d0n, d1n, x1_ref, wcol_ref, gate_ref, ys_hbm, *refs):
    params, (xout_ref, zrest_ref, q_ref, k_ref, v_ref), (g0, g1, sem) = refs[:13], refs[13:18], refs[18:]
    i = pl.program_id(0)
    nt = pl.num_programs(0)
    slot = i % 2
    tm = x1_ref.shape[0]

    def copies(d0_ref, d1_ref, s, r):
        return (_row_dma(ys_hbm, d0_ref[0, 0, r], g0.at[s], r, sem.at[s]),
                _row_dma(ys_hbm, d1_ref[0, 0, r], g1.at[s], r, sem.at[s]))

    def start(d0_ref, d1_ref, s, r):
        a, b = copies(d0_ref, d1_ref, s, r)
        a.start(priority=0)
        b.start(priority=1)

    def wait(d0_ref, d1_ref, s, r):
        a, b = copies(d0_ref, d1_ref, s, r)
        a.wait()
        b.wait()

    @pl.when(i == 0)
    def _():
        _loop(0, tm, lambda r: start(d0c, d1c, 0, r), unroll=8)

    _loop(0, tm, lambda r: wait(d0c, d1c, slot, r), unroll=8)
    w = wcol_ref[...]
    x = x1_ref[...] + gate_ref[0] * (w[:, 0:1] * g0[slot] + w[:, 1:2] * g1[slot])
    xout_ref[...] = x

    def prefetch():
        for r in range(tm):
            start(d0n, d1n, 1 - slot, r)

    _mixer_in_body(x, *params, zrest_ref, q_ref, k_ref, v_ref, before_last_store=prefetch)

    @pl.when(i == nt - 1)
    def _():
        _loop(0, tm, lambda r: wait(d0n, d1n, 1 - slot, r), unroll=8)


def _mixer_in(x, sh, sc, g1, win, qag, wuq, kvag, wk, wv, qg, kg, ctab, stab, seq, tm, pending=None):
    n = x.shape[0] if pending is None else pending[0].shape[0]
    nt = n // tm
    tpb = seq // tm
    hw = N_HEADS * HEAD_PAD
    row = lambda i: (i, 0)
    fixed = lambda i: (0, 0)
    per_b = lambda i: (i // tpb, 0, 0)
    param_specs = [pl.BlockSpec((1, 1, D_MODEL), per_b),
                   pl.BlockSpec((1, 1, D_MODEL), per_b),
                   pl.BlockSpec((1, D_MODEL), fixed),
                   pl.BlockSpec((D_MODEL, Z_COLS), fixed),
                   pl.BlockSpec((1, 256), fixed),
                   pl.BlockSpec((256, hw), fixed),
                   pl.BlockSpec((1, MLA_KV_LORA), fixed),
                   pl.BlockSpec((MLA_KV_LORA, hw), fixed),
                   pl.BlockSpec((MLA_KV_LORA, hw), fixed),
                   pl.BlockSpec((1, HEAD_PAD), fixed),
                   pl.BlockSpec((1, HEAD_PAD), fixed),
                   pl.BlockSpec((tm, HEAD_PAD), row),
                   pl.BlockSpec((tm, HEAD_PAD), row)]
    params = (sh, sc, g1, win, qag, wuq, kvag, wk, wv, qg, kg, ctab, stab)
    out_shape = (jax.ShapeDtypeStruct((n, Z_REST), f32),
                 jax.ShapeDtypeStruct((n, hw), bf16),
                 jax.ShapeDtypeStruct((n, hw), bf16),
                 jax.ShapeDtypeStruct((n, hw), bf16))
    out_specs = (pl.BlockSpec((tm, Z_REST), row),
                 pl.BlockSpec((tm, hw), row),
                 pl.BlockSpec((tm, hw), row),
                 pl.BlockSpec((tm, hw), row))
    if pending is None:
        return (x,) + tuple(pl.pallas_call(
            _mixer_in_kernel,
            out_shape=out_shape,
            grid=(nt,),
            in_specs=[pl.BlockSpec((tm, D_MODEL), row)] + param_specs,
            out_specs=out_specs,
            compiler_params=_cparams(("arbitrary",)),
            name="mixer_in",
        )(x, *params))
    x1, wcol, gate, ys, d0, d1 = pending
    cur = pl.BlockSpec((1, 1, tm), lambda i: (i, 0, 0), memory_space=pltpu.SMEM)
    nxt = pl.BlockSpec((1, 1, tm), lambda i: (jnp.minimum(i + 1, nt - 1), 0, 0), memory_space=pltpu.SMEM)
    return pl.pallas_call(
        _mixer_in_moe_kernel,
        out_shape=(jax.ShapeDtypeStruct((n, D_MODEL), f32),) + out_shape,
        grid=(nt,),
        in_specs=[cur, cur, nxt, nxt,
                  pl.BlockSpec((tm, D_MODEL), row),
                  pl.BlockSpec((tm, 8), row),
                  pl.BlockSpec((1, 1, D_MODEL), per_b),
                  pl.BlockSpec(memory_space=pl.ANY)] + param_specs,
        out_specs=(pl.BlockSpec((tm, D_MODEL), row),) + out_specs,
        scratch_shapes=[pltpu.VMEM((2, tm, D_MODEL), f32), pltpu.VMEM((2, tm, D_MODEL), f32),
                        pltpu.SemaphoreType.DMA((2,))],
        compiler_params=_cparams(("arbitrary",)),
        name="mixer_in_moe",
    )(d0, d1, d0, d1, x1, wcol, gate, ys, *params)


def _attn_kernel(q_ref, k_ref, v_ref, o_ref, m_scr, acc_scr, *, tq):
    qi = pl.program_id(2)
    q = q_ref[0]
    m_scr[...] = jnp.full(m_scr.shape, -jnp.inf, f32)
    acc_scr[...] = jnp.zeros(acc_scr.shape, f32)
    def block(qrows, m_prev, acc, start, size, masked):
        kb = k_ref[0, pl.ds(start, size), :]
        vb = v_ref[0, pl.ds(start, size), :]
        s = _nt(qrows, kb)
        if masked:
            r = lax.broadcasted_iota(i32, s.shape, 0)
            c = lax.broadcasted_iota(i32, s.shape, 1)
            s = jnp.where(c <= r, s, -1e30)
        m_new = jnp.maximum(m_prev, jnp.max(s, axis=-1, keepdims=True))
        alpha = jnp.exp2(m_prev - m_new)
        p = jnp.exp2(s - jnp.concatenate([m_new] * (size // HEAD_PAD), axis=1))
        return m_new, alpha * acc + _dot(p.astype(bf16), vb)

    def steps(js):
        m, acc = m_scr[...], acc_scr[...]
        for j in js:
            m, acc = block(q, m, acc, pl.multiple_of(j * tq, tq), tq, False)
        m_scr[...] = m
        acc_scr[...] = acc

    def quad(jq, carry):
        steps(tuple(4 * jq + i for i in range(4)))
        return carry

    lax.fori_loop(0, qi // 4, quad, 0)
    base = (qi // 4) * 4

    @pl.when(qi % 4 >= 2)
    def _():
        steps((base, base + 1))

    @pl.when(qi % 2 == 1)
    def _():
        steps((qi - 1,))

    _, acc = block(q, m_scr[...], acc_scr[...], pl.multiple_of(qi * tq, tq), tq, True)
    o_ref[0] = (acc / acc[:, MLA_V:MLA_V + 1]).astype(bf16)


def _attention(q, k, v, tq):
    b, s, hw = q.shape
    qspec = pl.BlockSpec((1, tq, HEAD_PAD), lambda bi, h, qi: (bi, qi, h))
    kvspec = pl.BlockSpec((1, s, HEAD_PAD), lambda bi, h, qi: (bi, 0, h))
    return pl.pallas_call(
        functools.partial(_attn_kernel, tq=tq),
        out_shape=jax.ShapeDtypeStruct((b, s, hw), bf16),
        grid=(b, N_HEADS, s // tq),
        in_specs=[qspec, kvspec, kvspec],
        out_specs=qspec,
        scratch_shapes=[pltpu.VMEM((tq, HEAD_PAD), f32), pltpu.VMEM((tq, HEAD_PAD), f32)],
        compiler_params=_cparams(("arbitrary", "arbitrary", "arbitrary")),
        name="mla_attention",
    )(q, k, v)


def _seqmix_kernel(z_ref, wpool_ref, pscale_ref, loglb_ref, l1mlb_ref, oml_ref, og_ref, dww_ref, dwb_ref,
                   lng_ref, lnb_ref, le_ref, bones_ref, sel_ref, bdmask_ref, cmask_ref,
                   y_ref, pool_ext, conv_ext, st_scr, *, t):
    step = pl.program_id(1)
    gw = GROUP_WIDTH

    @pl.when(step == 0)
    def _():
        pool_ext[0:16, :] = jnp.zeros((16, gw), f32)
        conv_ext[0:32, :] = jnp.zeros((32, gw), f32)
        st_scr[...] = jnp.zeros(st_scr.shape, f32)

    u = z_ref[:, 0:gw]
    pool_ext[16:16 + t, :] = u
    ext = pool_ext[...]
    sums = {}
    acc = ext
    for w in (1, 2, 4, 8):
        acc = acc + pltpu.roll(acc, w, 0)
        sums[2 * w] = acc[16:16 + t]
    col = lax.broadcasted_iota(i32, (t, gw), 1)
    pos1 = (step * t + lax.broadcasted_iota(i32, (t, gw), 0) + 1).astype(f32)
    wsum = jnp.where(col < 64, sums[2], jnp.where(col < 128, sums[4], jnp.where(col < 192, sums[8], sums[16])))
    wlen = jnp.where(col < 64, 2.0, jnp.where(col < 128, 4.0, jnp.where(col < 192, 8.0, 16.0)))
    pooled = wsum / jnp.minimum(pos1, wlen) - u
    y_ref[:, 0:gw] = (_dot(pooled.astype(bf16), wpool_ref[...]) * pscale_ref[...]).astype(bf16)
    pool_ext[0:16, :] = pool_ext[t:t + 16, :]

    uc = z_ref[:, 5 * gw:6 * gw] * jax.nn.sigmoid(z_ref[:, 6 * gw:7 * gw])
    conv_ext[32:32 + t, :] = uc
    cacc = jnp.zeros((t, gw), f32) + dwb_ref[...]
    for b in range(8):
        n_a = len(range(b, CONV_K, 8))
        xb = conv_ext[pl.ds(32 - (CONV_K - 1) + b, t + 8 * (n_a - 1)), :]
        for a in range(n_a):
            j = 8 * a + b
            cacc = cacc + xb[8 * a:8 * a + t] * dww_ref[j:j + 1, :]
    mu = jnp.mean(cacc, axis=-1, keepdims=True)
    cen = cacc - mu
    var = jnp.mean(cen * cen, axis=-1, keepdims=True)
    cn = cen * lax.rsqrt(var + EPS) * lng_ref[...] + lnb_ref[...]
    y_ref[:, 2 * gw:3 * gw] = _silu(cn).astype(bf16)
    conv_ext[0:32, :] = conv_ext[t:t + 32, :]

    hq = z_ref[:, 1 * gw:2 * gw]
    hf = z_ref[:, 2 * gw:3 * gw]
    v = z_ref[:, 3 * gw:4 * gw]
    hg = z_ref[:, 4 * gw:5 * gw]
    q = _silu(hq)
    ls = jnp.minimum(hf, 0.0) - jnp.log1p(jnp.exp(-jnp.abs(hf)))
    x1 = loglb_ref[...]
    x2 = l1mlb_ref[...] + ls
    logf = jnp.maximum(x1, x2) + jnp.log1p(jnp.exp(-jnp.abs(x1 - x2)))
    kk = oml_ref[...] * jax.nn.sigmoid(-hf)
    le = le_ref[...]
    hi, mid, lo = _split3(logf)
    cs = (_dot(le, hi) + _dot(le, mid) + _dot(le, lo)) * math.log2(math.e)
    bl = cs[0:t]
    be = cs[t:2 * t]
    qt = (q * jnp.exp2(bl)).astype(bf16)
    kt = (kk * jnp.exp2(be - bl)).astype(bf16)
    dec = jnp.exp2(be)
    vb = v.astype(bf16)
    bones = bones_ref[...]
    sel = sel_ref[...]
    bdmask = bdmask_ref[...]
    cmask = cmask_ref[...] > 0.0
    half = SUB // 2
    hw2 = gw // 2

    def by_t(x):
        rep = lambda a, nt: jnp.broadcast_to(a[:, None, :], (nt, half, gw)).reshape(nt * half, gw)
        return jnp.concatenate([rep(x, SUB), rep(x[half:SUB], half)], axis=0)

    def by_s(x):
        tile = lambda a, nt: jnp.broadcast_to(a[None, :, :], (nt, half, gw)).reshape(nt * half, gw)
        return jnp.concatenate([tile(x[0:half], SUB), tile(x[half:SUB], half)], axis=0)

    st = [st_scr[0], st_scr[1]]
    outs = []
    for j in range(t // SUB):
        rows = slice(j * SUB, (j + 1) * SUB)
        diff = by_t(bl[rows]) - by_s(bl[rows])
        e = jnp.exp2(jnp.where(cmask, diff, -1e30))
        p = (by_t(q[rows]) * by_s(kk[rows]) * e).astype(bf16)
        a = _dot(p, bones)
        w = (a * by_s(v[rows])).astype(bf16)
        o_j = _dot(sel, w)
        o_inter = []
        for hp in range(2):
            cols = slice(hp * hw2, (hp + 1) * hw2)
            o_inter.append(_nt(qt[rows, cols], st[hp].astype(bf16)))
            st[hp] = st[hp] * dec[j * SUB:j * SUB + 1, cols] + bdmask * _tn(vb[rows, cols], kt[rows, cols])
        outs.append(o_j + jnp.concatenate(o_inter, axis=1))
    st_scr[0] = st[0]
    st_scr[1] = st[1]
    o = jnp.concatenate(outs, axis=0)
    o2 = o * o
    ohi = o2.astype(bf16)
    olo = (o2 - ohi.astype(f32)).astype(bf16)
    msq = (_dot(ohi, bones) + _dot(olo, bones)) * (1.0 / HGRN_D)
    on = o * lax.rsqrt(msq + EPS) * og_ref[...]
    y_ref[:, gw:2 * gw] = (on * _silu(hg)).astype(bf16)


def _seqmix_consts(t):
    r = jnp.arange(t)
    same = (r[:, None] // SUB) == (r[None, :] // SUB)
    ltri = same & (r[None, :] <= r[:, None])
    le = jnp.concatenate([ltri, same], axis=0).astype(bf16)
    c = jnp.arange(GROUP_WIDTH)
    bd = (c[:, None] // HGRN_D) == (c[None, :] // HGRN_D)
    bones = bd.astype(bf16)
    bdmask = bd[:GROUP_WIDTH // 2, :GROUP_WIDTH // 2].astype(f32)
    half = SUB // 2
    r = jnp.arange(KEPT)
    first = r < SUB * half
    t_of = jnp.where(first, r // half, half + (r - SUB * half) // half)
    s_of = jnp.where(first, r % half, half + r % half)
    sel = (t_of[None, :] == jnp.arange(SUB)[:, None]).astype(bf16)
    cmask = (s_of <= t_of).astype(f32)[:, None] * jnp.ones((1, GROUP_WIDTH), f32)
    return le, bones, sel, bdmask, cmask


def _seqmix(zrest, wpool, pscale, loglb, l1mlb, oml, og, dww, dwb, lng, lnb, consts, batch, seq, t):
    n = zrest.shape[0]
    spb = seq // t
    le, bones, sel, bdmask, cmask = consts
    row = lambda b, s: (b * spb + s, 0)
    fixed = lambda b, s: (0, 0)
    vec = pl.BlockSpec((1, GROUP_WIDTH), fixed)
    return pl.pallas_call(
        functools.partial(_seqmix_kernel, t=t),
        out_shape=jax.ShapeDtypeStruct((n, 3 * GROUP_WIDTH), bf16),
        grid=(batch, spb),
        in_specs=[pl.BlockSpec((t, Z_REST), row),
                  pl.BlockSpec((GROUP_WIDTH, GROUP_WIDTH), fixed),
                  vec, vec, vec, vec, vec,
                  pl.BlockSpec((32, GROUP_WIDTH), fixed),
                  vec, vec, vec,
                  pl.BlockSpec((2 * t, t), fixed),
                  pl.BlockSpec((GROUP_WIDTH, GROUP_WIDTH), fixed),
                  pl.BlockSpec((SUB, KEPT), fixed),
                  pl.BlockSpec((GROUP_WIDTH // 2, GROUP_WIDTH // 2), fixed),
                  pl.BlockSpec((KEPT, GROUP_WIDTH), fixed)],
        out_specs=pl.BlockSpec((t, 3 * GROUP_WIDTH), row),
        scratch_shapes=[pltpu.VMEM((t + 16, GROUP_WIDTH), f32),
                        pltpu.VMEM((t + 32, GROUP_WIDTH), f32),
                        pltpu.VMEM((2, GROUP_WIDTH // 2, GROUP_WIDTH // 2), f32)],
        compiler_params=_cparams(("arbitrary", "arbitrary")),
        name="seq_mixers",
    )(zrest, wpool, pscale, loglb, l1mlb, oml, og, dww, dwb, lng, lnb, le, bones, sel, bdmask, cmask)


def _mixer_out_kernel(o_ref, y_ref, x_ref, gate_ref, wout_ref, n2g_ref, sc_ref, sh_ref, wr_ref, rb_ref,
                      x1_ref, h2_ref, eid_ref, wcol_ref, cnt_ref):
    hw = N_HEADS * HEAD_PAD
    wr = wr_ref[...]
    whi = wr.astype(bf16)
    wlo = (wr - whi.astype(f32)).astype(bf16)

    @pl.when(pl.program_id(0) == 0)
    def _():
        cnt_ref[...] = jnp.zeros(cnt_ref.shape, f32)

    tm = x_ref.shape[0]
    tmix = _dot(o_ref[...], wout_ref[0:hw, :]) + _dot(y_ref[...], wout_ref[hw:, :])
    x1 = x_ref[...] + gate_ref[0] * tmix
    x1_ref[...] = x1
    ms = jnp.mean(x1 * x1, axis=-1, keepdims=True)
    h2 = x1 * lax.rsqrt(ms + EPS) * n2g_ref[...]
    h2 = h2 * (1.0 + sc_ref[0]) + sh_ref[0]
    hhi = h2.astype(bf16)
    bits = pltpu.bitcast(hhi.astype(f32), u32)
    h2_ref[...] = (bits[:, D_MODEL // 2:] & jnp.uint32(0xFFFF0000)) | (bits[:, :D_MODEL // 2] >> 16)
    hlo = (h2 - hhi.astype(f32)).astype(bf16)
    lg = _nt(whi, hhi) + _nt(whi, hlo) + _nt(wlo, hhi) + rb_ref[...]
    g = [lg[i:i + 1, :] for i in range(N_GROUPS)]
    gmax = jnp.maximum(jnp.maximum(g[0], g[1]), jnp.maximum(g[2], g[3]))
    gidx = jnp.where(g[0] == gmax, 0, jnp.where(g[1] == gmax, 1, jnp.where(g[2] == gmax, 2, 3)))
    gsum = jnp.exp(g[0] - gmax) + jnp.exp(g[1] - gmax) + jnp.exp(g[2] - gmax) + jnp.exp(g[3] - gmax)
    gw_ = 1.0 / gsum
    e = [lg[8 + EXPERTS_PER_GROUP * i:8 + EXPERTS_PER_GROUP * (i + 1), :] for i in range(N_GROUPS)]
    esel = jnp.where(gidx == 0, e[0], jnp.where(gidx == 1, e[1], jnp.where(gidx == 2, e[2], e[3])))
    ri = lax.broadcasted_iota(i32, (EXPERTS_PER_GROUP, tm), 0)
    top1 = jnp.max(esel, axis=0, keepdims=True)
    idx1 = jnp.min(jnp.where(esel == top1, ri, EXPERTS_PER_GROUP), axis=0, keepdims=True)
    rest = jnp.where(ri == idx1, -jnp.inf, esel)
    top2 = jnp.max(rest, axis=0, keepdims=True)
    idx2 = jnp.min(jnp.where(rest == top2, ri, EXPERTS_PER_GROUP), axis=0, keepdims=True)
    e2 = jnp.exp(top2 - top1)
    den = 1.0 + e2
    ex0 = gidx * EXPERTS_PER_GROUP + idx1
    ex1 = gidx * EXPERTS_PER_GROUP + idx2
    eid_ref[...] = jnp.concatenate([ex0, ex1, jnp.zeros((6, tm), i32)], axis=0)
    ew8 = jnp.concatenate([(1.0 / den) * gw_, (e2 / den) * gw_, jnp.zeros((6, tm), f32)], axis=0)
    wcol_ref[...] = ew8.T
    rows = lax.broadcasted_iota(i32, (N_EXPERTS, tm), 0)
    onehot = jnp.where(rows == ex0, 1.0, 0.0) + jnp.where(rows == ex1, 1.0, 0.0)
    cnt_ref[...] += _dot(onehot.astype(bf16), jnp.ones((tm, 128), bf16))


def _mixer_out(o, y, x, gate, wout, n2g, sc, sh, wr, rb, seq, tm):
    n = x.shape[0]
    tpb = seq // tm
    hw = N_HEADS * HEAD_PAD
    row = lambda i: (i, 0)
    fixed = lambda i: (0, 0)
    per_b = lambda i: (i // tpb, 0, 0)
    colblk = lambda i: (0, i)
    return pl.pallas_call(
        _mixer_out_kernel,
        out_shape=(jax.ShapeDtypeStruct((n, D_MODEL), f32),
                   jax.ShapeDtypeStruct((n, D_MODEL // 2), u32),
                   jax.ShapeDtypeStruct((8, n), i32),
                   jax.ShapeDtypeStruct((n, 8), f32),
                   jax.ShapeDtypeStruct((N_EXPERTS, 128), f32)),
        grid=(n // tm,),
        in_specs=[pl.BlockSpec((tm, hw), row),
                  pl.BlockSpec((tm, 3 * GROUP_WIDTH), row),
                  pl.BlockSpec((tm, D_MODEL), row),
                  pl.BlockSpec((1, 1, D_MODEL), per_b),
                  pl.BlockSpec((hw + 3 * GROUP_WIDTH, D_MODEL), fixed),
                  pl.BlockSpec((1, D_MODEL), fixed),
                  pl.BlockSpec((1, 1, D_MODEL), per_b),
                  pl.BlockSpec((1, 1, D_MODEL), per_b),
                  pl.BlockSpec((ROUTER_ROWS, D_MODEL), fixed),
                  pl.BlockSpec((ROUTER_ROWS, 1), fixed)],
        out_specs=(pl.BlockSpec((tm, D_MODEL), row),
                   pl.BlockSpec((tm, D_MODEL // 2), row),
                   pl.BlockSpec((8, tm), colblk),
                   pl.BlockSpec((tm, 8), row),
                   pl.BlockSpec((N_EXPERTS, 128), fixed)),
        compiler_params=_cparams(("arbitrary",)),
        name="mixer_out_router",
    )(o, y, x, gate, wout, n2g, sc, sh, wr, rb)


def _moe_rank_kernel(eid_ref, pstart_ref, u_ref, dest_ref, prefix):
    @pl.when(pl.program_id(0) == 0)
    def _():
        prefix[...] = jnp.zeros(prefix.shape, f32)

    tm = eid_ref.shape[1]
    rows = lax.broadcasted_iota(i32, (N_EXPERTS, tm), 0)
    oh0 = jnp.where(rows == eid_ref[0:1, :], 1.0, 0.0)
    oh1 = jnp.where(rows == eid_ref[1:2, :], 1.0, 0.0)
    u = u_ref[...]
    cs0 = _dot(oh0.astype(bf16), u)
    cs1 = _dot(oh1.astype(bf16), u)
    tot0 = cs0[:, tm - 1:tm]
    tot1 = cs1[:, tm - 1:tm]
    base = prefix[...] + pstart_ref[...]
    d0 = jnp.sum(oh0 * (base + cs0 - 1.0), axis=0, keepdims=True)
    d1 = jnp.sum(oh1 * (base + tot0 + cs1 - 1.0), axis=0, keepdims=True)
    prefix[...] = prefix[...] + tot0 + tot1
    dest_ref[...] = jnp.concatenate([d0.astype(i32), d1.astype(i32), jnp.zeros((6, tm), i32)], axis=0)


def _moe_rank(eid, pstart, tm):
    n = eid.shape[1]
    r = jnp.arange(tm)
    u = (r[:, None] <= r[None, :]).astype(bf16)
    return pl.pallas_call(
        _moe_rank_kernel,
        out_shape=jax.ShapeDtypeStruct((8, n), i32),
        grid=(n // tm,),
        in_specs=[pl.BlockSpec((8, tm), lambda i: (0, i)),
                  pl.BlockSpec((N_EXPERTS, 1), lambda i: (0, 0)),
                  pl.BlockSpec((tm, tm), lambda i: (0, 0))],
        out_specs=pl.BlockSpec((8, tm), lambda i: (0, i)),
        scratch_shapes=[pltpu.VMEM((N_EXPERTS, 1), f32)],
        compiler_params=_cparams(("arbitrary",)),
        name="moe_rank",
    )(eid, pstart, u)


def _row_dma(src, src_row, dst, dst_row, sem):
    return pltpu.make_async_copy(src.at[pl.ds(src_row, 1), :], dst.at[pl.ds(dst_row, 1), :], sem)


def _loop(lo, hi, fn, unroll=1):
    def body(r, c):
        fn(r)
        return c
    lax.fori_loop(lo, hi, body, 0, unroll=unroll)


FFN_SLOTS = 3


def _ffn_kernel(be_ref, nu_ref, src0, src1, src2, w1_ref, w3_ref, w2_ref, h_hbm, ys_ref, xbuf, gsem,
                w1b, w3b, w2b):
    i = pl.program_id(0)
    nb = pl.num_programs(0)
    nu = nu_ref[0]
    used = i < nu

    slot0 = i % FFN_SLOTS
    slot1 = (i + 1) % FFN_SLOTS
    slot2 = (i + 2) % FFN_SLOTS

    def row_copy(idx_ref, s, r):
        return _row_dma(h_hbm, idx_ref[0, 0, r], xbuf.at[s], r, gsem.at[s])

    def start_rows(idx_ref, s):
        _loop(0, MOE_BLOCK, lambda r: row_copy(idx_ref, s, r).start(), unroll=8)

    def wait_rows(idx_ref, s):
        _loop(0, MOE_BLOCK, lambda r: row_copy(idx_ref, s, r).wait(), unroll=8)

    @pl.when(i == 0)
    def _():
        start_rows(src0, 0)
        start_rows(src1, 1)

    changed = jnp.logical_or(i == 0, be_ref[i] != be_ref[jnp.maximum(i - 1, 0)])

    @pl.when(jnp.logical_and(used, changed))
    def _():
        w1b[...] = w1_ref[0, 0].astype(bf16)
        w3b[...] = w3_ref[0, 0].astype(bf16)
        w2b[...] = w2_ref[0, 0].astype(bf16)

    @pl.when(used)
    def _():
        wait_rows(src0, slot0)
        packed = xbuf[slot0]
        lo = pltpu.bitcast(packed << 16, f32)
        hi = pltpu.bitcast(packed & jnp.uint32(0xFFFF0000), f32)
        xb = jnp.concatenate([lo, hi], axis=1).astype(bf16)
        a = _dot(xb, w1b[...])
        b = _dot(xb, w3b[...])
        hb = (_silu(a) * b).astype(bf16)
        y = _dot(hb, w2b[...])
        for r in range(MOE_BLOCK):
            row_copy(src2, slot2, r).start(priority=r % 2)
        ys_ref[...] = y

    @pl.when(jnp.logical_not(used))
    def _():
        @pl.when(i < nu + 2)
        def _():
            wait_rows(src0, slot0)

        ys_ref[...] = jnp.zeros(ys_ref.shape, f32)

    @pl.when(i == nb - 1)
    def _():
        @pl.when(i - 1 < nu)
        def _():
            wait_rows(src1, slot1)

        @pl.when(used)
        def _():
            wait_rows(src2, slot2)


def _moe_ffn(h2p, row_src, blk_expert, n_used, w1, w3, w2, layer):
    n_rows = row_src.shape[0]
    nb = n_rows // MOE_BLOCK
    src3 = jnp.concatenate([row_src, jnp.zeros((2 * MOE_BLOCK,), i32)]).reshape(nb + 2, 1, MOE_BLOCK)
    smem_blk = lambda ahead: pl.BlockSpec((1, 1, MOE_BLOCK), lambda i, be, nu: (i + ahead, 0, 0),
                                          memory_space=pltpu.SMEM)
    wblk = lambda i, be, nu: (layer, be[jnp.minimum(i, nu[0] - 1)], 0, 0)
    grid_spec = pltpu.PrefetchScalarGridSpec(
        num_scalar_prefetch=2,
        grid=(nb,),
        in_specs=[smem_blk(0), smem_blk(1), smem_blk(2),
                  pl.BlockSpec((1, 1, D_MODEL, D_EXPERT), wblk),
                  pl.BlockSpec((1, 1, D_MODEL, D_EXPERT), wblk),
                  pl.BlockSpec((1, 1, D_EXPERT, D_MODEL), wblk),
                  pl.BlockSpec(memory_space=pl.ANY)],
        out_specs=pl.BlockSpec((MOE_BLOCK, D_MODEL), lambda i, be, nu: (i, 0)),
        scratch_shapes=[pltpu.VMEM((FFN_SLOTS, MOE_BLOCK, D_MODEL // 2), u32),
                        pltpu.SemaphoreType.DMA((FFN_SLOTS,)),
                        pltpu.VMEM((D_MODEL, D_EXPERT), bf16),
                        pltpu.VMEM((D_MODEL, D_EXPERT), bf16),
                        pltpu.VMEM((D_EXPERT, D_MODEL), bf16)])
    return pl.pallas_call(
        _ffn_kernel,
        out_shape=jax.ShapeDtypeStruct((n_rows, D_MODEL), f32),
        grid_spec=grid_spec,
        compiler_params=_cparams(("arbitrary",)),
        name="moe_ffn",
    )(blk_expert, n_used, src3, src3, src3, w1, w3, w2, h2p)


def _combine_kernel(d0c, d1c, d0n, d1n, x_ref, wcol_ref, gate_ref, ys_hbm, o_ref, g0, g1, sem):
    i = pl.program_id(0)
    nb = pl.num_programs(0)
    slot = i % 2
    tm = x_ref.shape[0]

    def copies(d0_ref, d1_ref, s, r):
        return (_row_dma(ys_hbm, d0_ref[0, 0, r], g0.at[s], r, sem.at[s]),
                _row_dma(ys_hbm, d1_ref[0, 0, r], g1.at[s], r, sem.at[s]))

    def start(d0_ref, d1_ref, s):
        def fn(r):
            a, b = copies(d0_ref, d1_ref, s, r)
            a.start(priority=0)
            b.start(priority=1)
        _loop(0, tm, fn, unroll=8)

    @pl.when(i == 0)
    def _():
        start(d0c, d1c, 0)

    @pl.when(i + 1 < nb)
    def _():
        start(d0n, d1n, 1 - slot)

    def wait(r):
        a, b = copies(d0c, d1c, slot, r)
        a.wait()
        b.wait()

    _loop(0, tm, wait, unroll=8)
    w = wcol_ref[...]
    y = w[:, 0:1] * g0[slot] + w[:, 1:2] * g1[slot]
    o_ref[...] = x_ref[...] + gate_ref[0] * y


def _combine(x1, wcol, gate, ys, d0, d1, seq, tm):
    n = x1.shape[0]
    nt = n // tm
    tpb = seq // tm
    cur = pl.BlockSpec((1, 1, tm), lambda i: (i, 0, 0), memory_space=pltpu.SMEM)
    nxt = pl.BlockSpec((1, 1, tm), lambda i: (jnp.minimum(i + 1, nt - 1), 0, 0), memory_space=pltpu.SMEM)
    return pl.pallas_call(
        _combine_kernel,
        out_shape=jax.ShapeDtypeStruct((n, D_MODEL), f32),
        grid=(nt,),
        in_specs=[cur, cur, nxt, nxt,
                  pl.BlockSpec((tm, D_MODEL), lambda i: (i, 0)),
                  pl.BlockSpec((tm, 8), lambda i: (i, 0)),
                  pl.BlockSpec((1, 1, D_MODEL), lambda i: (i // tpb, 0, 0)),
                  pl.BlockSpec(memory_space=pl.ANY)],
        out_specs=pl.BlockSpec((tm, D_MODEL), lambda i: (i, 0)),
        scratch_shapes=[pltpu.VMEM((2, tm, D_MODEL), f32), pltpu.VMEM((2, tm, D_MODEL), f32),
                        pltpu.SemaphoreType.DMA((2,))],
        compiler_params=_cparams(("arbitrary",)),
        name="moe_combine",
    )(d0, d1, d0, d1, x1, wcol, gate, ys)


def _moe(h2p, x1, eid, wcol, cnt, gate, w1, w3, w2, layer, seq, tm):
    n = h2p.shape[0]
    n_rows = 2 * n + N_EXPERTS * MOE_BLOCK
    nb = n_rows // MOE_BLOCK
    counts = cnt[:, 0].astype(i32)
    padded = (counts + MOE_BLOCK - 1) // MOE_BLOCK * MOE_BLOCK
    pend = jnp.cumsum(padded)
    pstart = pend - padded
    n_used = (pend[-1:] // MOE_BLOCK).astype(i32)
    blk_start = jnp.arange(nb, dtype=i32) * MOE_BLOCK
    blk_expert = jnp.minimum(jnp.sum(blk_start[:, None] >= pend[None, :], axis=1), N_EXPERTS - 1).astype(i32)
    dest = _moe_rank(eid, pstart.astype(f32)[:, None], min(512, n))
    by_dest = jnp.argsort(dest[0:2].reshape(-1)).astype(i32)
    first = jnp.cumsum(counts) - counts
    off = jnp.arange(n_rows, dtype=i32).reshape(nb, MOE_BLOCK) - pstart[blk_expert][:, None]
    valid = off < counts[blk_expert][:, None]
    pick = jnp.clip(first[blk_expert][:, None] + off, 0, 2 * n - 1)
    row_src = jnp.where(valid, by_dest[pick] % n, 0).astype(i32).reshape(n_rows)
    ys = _moe_ffn(h2p, row_src, blk_expert, n_used, w1, w3, w2, layer)
    d0 = dest[0].reshape(n // tm, 1, tm)
    d1 = dest[1].reshape(n // tm, 1, tm)
    return x1, wcol, gate, ys, d0, d1


def _pad_to(a, axis, size):
    pad = [(0, 0)] * a.ndim
    pad[axis] = (0, size - a.shape[axis])
    return jnp.pad(a, pad)


def _prep_layer_params(w_in, q_a_norm_g, w_uq, kv_a_norm_g, w_ukv, q_norm_g, k_norm_g, w_pool, pool_scale,
                       hgrn_lb_logits, hgrn_out_norm_g, conv_dw_w, w_out, router_group_w, router_group_b,
                       router_expert_w, router_expert_b):
    L = w_in.shape[0]
    zc = lambda n_: jnp.zeros((L, D_MODEL, n_), f32)
    win = jnp.concatenate([w_in[:, :, 352:], w_in[:, :, 0:192], zc(64), w_in[:, :, 192:320], zc(64),
                           w_in[:, :, 320:352], w_in[:, :, 320:336], zc(16)], axis=2).astype(bf16)
    qag = _pad_to(q_a_norm_g, 1, 256)[:, None, :]
    half = MLA_ROPE // 2
    x1 = slice(MLA_NOPE, MLA_NOPE + half)
    wuq = w_uq.reshape(L, MLA_Q_LORA, N_HEADS, MLA_QK)
    wuq = _pad_to(_pad_to(jnp.concatenate([wuq, wuq[..., x1]], axis=3), 3, HEAD_PAD), 1, 256)
    wuq = wuq.reshape(L, 256, N_HEADS * HEAD_PAD).astype(bf16)
    kvag = kv_a_norm_g[:, None, :]
    wkv = w_ukv.reshape(L, MLA_KV_LORA, N_HEADS, MLA_NOPE + MLA_V)
    wk = _pad_to(wkv[..., :MLA_NOPE], 3, HEAD_PAD).reshape(L, MLA_KV_LORA, N_HEADS * HEAD_PAD).astype(bf16)
    wv = _pad_to(wkv[..., MLA_NOPE:], 3, HEAD_PAD).reshape(L, MLA_KV_LORA, N_HEADS * HEAD_PAD).astype(bf16)
    qg = _pad_to(jnp.concatenate([q_norm_g, q_norm_g[:, x1]], axis=1), 1, HEAD_PAD)[:, None, :]
    kg = _pad_to(jnp.concatenate([k_norm_g, k_norm_g[:, x1]], axis=1), 1, HEAD_PAD)[:, None, :]
    wpool = jnp.zeros((L, GROUP_WIDTH, GROUP_WIDTH), f32)
    for g in range(len(POOL_WINDOWS)):
        wpool = wpool.at[:, 64 * g:64 * (g + 1), 64 * g:64 * (g + 1)].set(w_pool[:, g])
    wpool = wpool.astype(bf16)
    lb_cum = jnp.cumsum(jax.nn.softmax(hgrn_lb_logits.astype(f32), axis=0), axis=0)
    lb = lb_cum - lb_cum[0:1]
    loglb = jnp.log(lb)[:, None, :]
    l1mlb = jnp.log1p(-lb)[:, None, :]
    oml = (1.0 - lb)[:, None, :]
    og = jnp.tile(hgrn_out_norm_g, (1, N_HEADS))[:, None, :]
    dww = _pad_to(conv_dw_w, 1, 32)
    wo_attn = _pad_to(w_out[:, :GROUP_WIDTH].reshape(L, N_HEADS, MLA_V, D_MODEL), 2, HEAD_PAD)
    wout = jnp.concatenate([wo_attn.reshape(L, N_HEADS * HEAD_PAD, D_MODEL), w_out[:, GROUP_WIDTH:]], axis=1).astype(bf16)
    wr = jnp.concatenate([jnp.swapaxes(router_group_w, 1, 2), jnp.zeros((L, 4, D_MODEL), f32),
                          jnp.swapaxes(router_expert_w, 1, 2)], axis=1)
    rb = jnp.concatenate([router_group_b, jnp.zeros((L, 4), f32), router_expert_b], axis=1)[:, :, None]
    return dict(win=win, qag=qag, wuq=wuq, kvag=kvag, wk=wk, wv=wv, qg=qg, kg=kg, wpool=wpool,
                pscale=pool_scale[:, None, :], loglb=loglb, l1mlb=l1mlb, oml=oml, og=og, dww=dww,
                wout=wout, wr=wr, rb=rb)


def _rope_tabs(positions):
    half = MLA_ROPE // 2
    inv_freq = ROPE_THETA ** (-jnp.arange(half, dtype=f32) / half)
    ang = positions.astype(f32).reshape(-1)[:, None] * inv_freq
    cos = jnp.cos(ang)
    sin = jnp.sin(ang)
    n = cos.shape[0]
    ctab = jnp.concatenate([jnp.ones((n, MLA_NOPE), f32), cos, cos, jnp.zeros((n, 32), f32)], axis=1)
    stab = jnp.concatenate([jnp.zeros((n, MLA_NOPE), f32), -sin, sin, jnp.zeros((n, 32), f32)], axis=1)
    return ctab, stab


def kernel(x, c, positions, w_ada, b_ada, norm1_g, norm2_g, w_in, q_a_norm_g, w_uq, kv_a_norm_g, w_ukv, q_norm_g, k_norm_g, w_pool, pool_scale, hgrn_lb_logits, hgrn_out_norm_g, conv_dw_w, conv_dw_b, conv_ln_g, conv_ln_b, w_out, router_group_w, router_group_b, router_expert_w, router_expert_b, w1, w3, w2):
    B, S, D = x.shape
    L = w_in.shape[0]
    n = B * S
    tm, tq, tseq = _tiles(S)

    p = _prep_layer_params(w_in, q_a_norm_g, w_uq, kv_a_norm_g, w_ukv, q_norm_g, k_norm_g, w_pool, pool_scale,
                           hgrn_lb_logits, hgrn_out_norm_g, conv_dw_w, w_out, router_group_w, router_group_b,
                           router_expert_w, router_expert_b)
    ctab, stab = _rope_tabs(positions)
    consts = _seqmix_consts(tseq)
    mod = _ada_mod(c, w_ada, b_ada)
    xf = x.reshape(n, D)
    hw = N_HEADS * HEAD_PAD
    pending = None
    for l in range(L):
        m6 = [mod[l, :, i * D:(i + 1) * D][:, None, :] for i in range(6)]
        sh1, sc1, g1, sh2, sc2, g2 = m6
        xf, zrest, q, k, v = _mixer_in(xf, sh1, sc1, norm1_g[l][None, :], p['win'][l], p['qag'][l], p['wuq'][l],
                                       p['kvag'][l], p['wk'][l], p['wv'][l], p['qg'][l], p['kg'][l],
                                       ctab, stab, S, tm, pending)
        o = _attention(q.reshape(B, S, hw), k.reshape(B, S, hw), v.reshape(B, S, hw), tq).reshape(n, hw)
        y = _seqmix(zrest, p['wpool'][l], p['pscale'][l], p['loglb'][l], p['l1mlb'][l], p['oml'][l], p['og'][l],
                    p['dww'][l], conv_dw_b[l][None, :], conv_ln_g[l][None, :], conv_ln_b[l][None, :],
                    consts, B, S, tseq)
        x1, h2, eid, wcol, cnt = _mixer_out(o, y, xf, g1, p['wout'][l], norm2_g[l][None, :], sc2, sh2,
                                            p['wr'][l], p['rb'][l], S, tm)
        pending = _moe(h2, x1, eid, wcol, cnt, g2, w1, w3, w2, l, S, tm)
    return _combine(*pending, S, tm).reshape(B, S, D)
```
